```python
import jax, jax.numpy as jnp
from jax import lax
import numpy as np

D_MODEL = 4096
BATCH = 4
SEQ = 4096
DEPTH = 4

D_MIX = D_MODEL
SWA_WIDTH = D_MIX // 2
SWA_HEAD_DIM = 64
SWA_Q_HEADS = SWA_WIDTH // SWA_HEAD_DIM
SWA_KV_HEADS = max(SWA_Q_HEADS // 8, 1)
SWA_GROUP = SWA_Q_HEADS // SWA_KV_HEADS
WINDOW = 128
BLOCK = 128

GLA_WIDTH = D_MIX - SWA_WIDTH
GLA_HEADS = 4
GLA_KEY_WIDTH = GLA_WIDTH // 2
GLA_DK = GLA_KEY_WIDTH // GLA_HEADS
GLA_DV = GLA_WIDTH // GLA_HEADS
GLA_GATE_RANK = 16
GLA_TAU = 16.0
GLA_CHUNK = 64

COL_SIZES = (SWA_WIDTH,
             SWA_KV_HEADS * SWA_HEAD_DIM,
             SWA_KV_HEADS * SWA_HEAD_DIM,
             GLA_KEY_WIDTH,
             GLA_KEY_WIDTH,
             GLA_WIDTH,
             GLA_WIDTH,
             GLA_GATE_RANK)
IN_COLS = sum(COL_SIZES)
COL_SPLITS = [int(c) for c in np.cumsum(COL_SIZES)[:-1]]

D_FF_DENSE = 2 * D_MODEL
N_EXPERTS = 8
TOP_K = 2
D_FF_EXPERT = D_MODEL // 2
N_DENSE = (DEPTH + 1) // 2
N_MOE = DEPTH // 2

EPS = 1e-5

kernel_name = "hybrid_swa_gla_moe_trunk"


def rms_norm(x, g):
    xf = x.astype(jnp.float32)
    y = xf * lax.rsqrt(jnp.mean(xf * xf, axis=-1, keepdims=True) + EPS)
    return (y * g.astype(jnp.float32)).astype(x.dtype)


def alibi_slopes(n_heads):
    return 2.0 ** (-8.0 * jnp.arange(1, n_heads + 1, dtype=jnp.float32) / n_heads)


def sliding_window_attention(q, k, v, sinks):
    B, T, _, D = q.shape
    nb = T // BLOCK
    qb = q.reshape(B, nb, BLOCK, SWA_KV_HEADS, SWA_GROUP, D)

    def band(a):
        ap = jnp.pad(a, ((0, 0), (BLOCK, 0), (0, 0), (0, 0))).reshape(B, nb + 1, BLOCK, SWA_KV_HEADS, D)
        return jnp.concatenate([ap[:, :-1], ap[:, 1:]], axis=2)

    kb, vb = band(k), band(v)
    scores = jnp.einsum('bnqhgd,bnkhd->bhgnqk', qb, kb).astype(jnp.float32) * (D ** -0.5)

    qi = jnp.arange(BLOCK)
    kj = jnp.arange(2 * BLOCK)
    dist = (qi[:, None] - kj[None, :] + BLOCK).astype(jnp.float32)
    key_pos = (jnp.arange(nb)[:, None] - 1) * BLOCK + kj[None, :]
    mask = ((dist >= 0) & (dist < WINDOW))[None] & (key_pos >= 0)[:, None, :]

    slopes = alibi_slopes(SWA_Q_HEADS).reshape(SWA_KV_HEADS, SWA_GROUP)
    bias = -slopes[:, :, None, None, None] * dist[None, None, None]
    scores = jnp.where(mask, scores + bias, -jnp.inf)

    sink = jnp.broadcast_to(sinks.astype(jnp.float32).reshape(1, SWA_KV_HEADS, SWA_GROUP, 1, 1, 1),
                            scores.shape[:-1] + (1,))
    probs = jax.nn.softmax(jnp.concatenate([scores, sink], axis=-1), axis=-1)[..., :-1]
    out = jnp.einsum('bhgnqk,bnkhd->bnqhgd', probs.astype(v.dtype), vb)
    return out.reshape(B, T, SWA_Q_HEADS * D)


def gated_linear_attention(q, k, v, log_a):
    B, T, H, Dk = q.shape
    Dv = v.shape[-1]
    C = GLA_CHUNK
    N = T // C
    f32 = jnp.float32
    qc = q.astype(f32).reshape(B, N, C, H, Dk) * (Dk ** -0.5)
    kc = k.astype(f32).reshape(B, N, C, H, Dk)
    vc = v.astype(f32).reshape(B, N, C, H, Dv)
    b = jnp.cumsum(log_a.astype(f32).reshape(B, N, C, H, Dk), axis=2)
    b_last = b[:, :, -1:]

    q_dec = qc * jnp.exp(b)
    k_inv = kc * jnp.exp(-b)
    k_end = kc * jnp.exp(b_last - b)

    A = jnp.einsum('bnchk,bnshk->bnhcs', q_dec, k_inv)
    A = jnp.where(jnp.tril(jnp.ones((C, C), dtype=bool)), A, 0.0)
    o_intra = jnp.einsum('bnhcs,bnshv->bnchv', A, vc)

    def step(S, xs):
        qd, ke, vv, dec = xs
        o = jnp.einsum('bchk,bhkv->bchv', qd, S)
        S = S * dec[..., None] + jnp.einsum('bchk,bchv->bhkv', ke, vv)
        return S, o

    xs = (jnp.moveaxis(q_dec, 1, 0), jnp.moveaxis(k_end, 1, 0), jnp.moveaxis(vc, 1, 0),
          jnp.moveaxis(jnp.exp(b_last[:, :, 0]), 1, 0))
    S0 = jnp.zeros((B, H, Dk, Dv), f32)
    _, o_inter = lax.scan(step, S0, xs)
    o = o_intra + jnp.moveaxis(o_inter, 0, 1)
    return o.reshape(B, T, H, Dv).astype(v.dtype)


def hybrid_mixer(h, w_in, gla_gate_w, gla_gate_b, gla_norm_g, swa_sinks, swa_norm_g, w_out):
    B, T, _ = h.shape
    proj = h @ w_in
    sq, sk, sv, gq, gk, gv, gg, glr = jnp.split(proj, COL_SPLITS, axis=-1)

    swa = sliding_window_attention(sq.reshape(B, T, SWA_Q_HEADS, SWA_HEAD_DIM),
                                   sk.reshape(B, T, SWA_KV_HEADS, SWA_HEAD_DIM),
                                   sv.reshape(B, T, SWA_KV_HEADS, SWA_HEAD_DIM), swa_sinks)
    swa = rms_norm(swa, swa_norm_g)

    gate_logits = (glr @ gla_gate_w + gla_gate_b).astype(jnp.float32)
    log_a = (jax.nn.log_sigmoid(gate_logits) / GLA_TAU).reshape(B, T, GLA_HEADS, GLA_DK)
    o = gated_linear_attention(gq.reshape(B, T, GLA_HEADS, GLA_DK),
                               gk.reshape(B, T, GLA_HEADS, GLA_DK),
                               gv.reshape(B, T, GLA_HEADS, GLA_DV), log_a)
    o = rms_norm(o, gla_norm_g) * jax.nn.silu(gg.reshape(B, T, GLA_HEADS, GLA_DV))
    gla = o.reshape(B, T, GLA_WIDTH)

    return jnp.concatenate([swa, gla], axis=-1) @ w_out


def swiglu(x, w_gate, w_up, w_down):
    return (jax.nn.silu(x @ w_gate) * (x @ w_up)) @ w_down


def moe_swiglu(x, w_router, w_gate, w_up, w_down):
    B, T, D = x.shape
    xt = x.reshape(B * T, D)
    logits = (xt @ w_router).astype(jnp.float32)
    top_vals, top_idx = lax.top_k(logits, TOP_K)
    top_w = jax.nn.softmax(top_vals, axis=-1)
    gates = jnp.sum(jax.nn.one_hot(top_idx, N_EXPERTS, dtype=jnp.float32) * top_w[..., None], axis=1)
    out = jnp.zeros((B * T, D), jnp.float32)
    for e in range(N_EXPERTS):
        out = out + gates[:, e:e + 1] * swiglu(xt, w_gate[e], w_up[e], w_down[e]).astype(jnp.float32)
    return out.astype(x.dtype).reshape(B, T, D)


def setup_inputs(seed: int = 0) -> dict:
    key = jax.random.key(seed)
    ks = jax.random.split(key, 20)
    f32 = jnp.float32

    def w(k, shape, fan_in):
        return jax.random.normal(k, shape, f32) * (fan_in ** -0.5)

    def gain(k, shape):
        return 1.0 + 0.02 * jax.random.normal(k, shape, f32)

    return {
        "x": jax.random.normal(ks[0], (BATCH, SEQ, D_MODEL), f32),
        "attn_norm_g": gain(ks[1], (DEPTH, D_MODEL)),
        "w_in": w(ks[2], (DEPTH, D_MODEL, IN_COLS), D_MODEL),
        "gla_gate_w": w(ks[3], (DEPTH, GLA_GATE_RANK, GLA_KEY_WIDTH), GLA_GATE_RANK),
        "gla_gate_b": 0.1 * jax.random.normal(ks[4], (DEPTH, GLA_KEY_WIDTH), f32),
        "gla_norm_g": gain(ks[5], (DEPTH, GLA_DV)),
        "swa_sinks": 0.5 * jax.random.normal(ks[6], (DEPTH, SWA_Q_HEADS), f32),
        "swa_norm_g": gain(ks[7], (DEPTH, SWA_WIDTH)),
        "w_out": w(ks[8], (DEPTH, D_MIX, D_MODEL), D_MIX),
        "ffn_norm_g": gain(ks[9], (DEPTH, D_MODEL)),
        "dense_w_gate": w(ks[10], (N_DENSE, D_MODEL, D_FF_DENSE), D_MODEL),
        "dense_w_up": w(ks[11], (N_DENSE, D_MODEL, D_FF_DENSE), D_MODEL),
        "dense_w_down": w(ks[12], (N_DENSE, D_FF_DENSE, D_MODEL), D_FF_DENSE),
        "moe_router": w(ks[13], (N_MOE, D_MODEL, N_EXPERTS), D_MODEL),
        "moe_w_gate": w(ks[14], (N_MOE, N_EXPERTS, D_MODEL, D_FF_EXPERT), D_MODEL),
        "moe_w_up": w(ks[15], (N_MOE, N_EXPERTS, D_MODEL, D_FF_EXPERT), D_MODEL),
        "moe_w_down": w(ks[16], (N_MOE, N_EXPERTS, D_FF_EXPERT, D_MODEL), D_FF_EXPERT),
        "final_norm_g": gain(ks[17], (D_MODEL,)),
    }


def reference(x, attn_norm_g, w_in, gla_gate_w, gla_gate_b, gla_norm_g, swa_sinks, swa_norm_g,
              w_out, ffn_norm_g, dense_w_gate, dense_w_up, dense_w_down, moe_router,
              moe_w_gate, moe_w_up, moe_w_down, final_norm_g):
    for layer in range(DEPTH):
        h = x + hybrid_mixer(rms_norm(x, attn_norm_g[layer]), w_in[layer], gla_gate_w[layer],
                             gla_gate_b[layer], gla_norm_g[layer], swa_sinks[layer],
                             swa_norm_g[layer], w_out[layer])
        n = rms_norm(h, ffn_norm_g[layer])
        j = layer // 2
        if layer % 2 == 0:
            f = swiglu(n, dense_w_gate[j], dense_w_up[j], dense_w_down[j])
        else:
            f = moe_swiglu(n, moe_router[j], moe_w_gate[j], moe_w_up[j], moe_w_down[j])
        x = h + f
    return rms_norm(x, final_norm_g)
```

```python
import functools

import jax
import jax.numpy as jnp
import numpy as np
from jax import lax
from jax.experimental import pallas as pl
from jax.experimental.pallas import tpu as pltpu

BF16 = jnp.bfloat16
F32 = jnp.float32

D_MODEL = 4096
DEPTH = 4
EPS = 1e-5

SWA_WIDTH = 2048
SWA_HEAD_DIM = 64
SWA_Q_HEADS = 32
SWA_KV_HEADS = 4
SWA_GROUP = 8
SWA_KV_WIDTH = SWA_KV_HEADS * SWA_HEAD_DIM
WINDOW = 128

GLA_WIDTH = 2048
GLA_HEADS = 4
GLA_KEY_WIDTH = 1024
GLA_DK = 256
GLA_DV = 512
GLA_GATE_RANK = 16
GLA_TAU = 16.0
GLA_CHUNK = 64

D_FF_DENSE = 8192
N_EXPERTS = 8
D_FF_EXPERT = 2048

LANES = 128
VMEM_LIMIT = 56 * 1024 * 1024

COL_SQ = 0
COL_GV = 2048
COL_GG = 4096
COL_GQ = 6144
COL_GK = 7168
COL_SK = 8192
COL_SV = 8448
COL_GLR = 8704
PROJ_COLS = 9216

MM_TILE = 1024
NEG_BIG = -1e30


def _params(*sem):
    return pltpu.CompilerParams(dimension_semantics=sem, vmem_limit_bytes=VMEM_LIMIT)


def _rmsnorm_kernel(x_ref, g_ref, o_ref):
    x = x_ref[...]
    r = lax.rsqrt(jnp.mean(x * x, axis=-1, keepdims=True) + EPS)
    o_ref[...] = (x * r * g_ref[...]).astype(o_ref.dtype)


def rmsnorm(x, g, out_dtype, tm=512):
    n, d = x.shape
    return pl.pallas_call(
        _rmsnorm_kernel,
        out_shape=jax.ShapeDtypeStruct((n, d), out_dtype),
        grid=(n // tm,),
        in_specs=[pl.BlockSpec((tm, d), lambda i: (i, 0)),
                  pl.BlockSpec((1, d), lambda i: (0, 0))],
        out_specs=pl.BlockSpec((tm, d), lambda i: (i, 0)),
        compiler_params=_params("parallel"),
        name="rmsnorm",
    )(x, g.reshape(1, d))


def _rmsnorm_router_kernel(x_ref, g_ref, wr_ref, o_ref, gates_ref):
    x = x_ref[...]
    r = lax.rsqrt(jnp.mean(x * x, axis=-1, keepdims=True) + EPS)
    n = (x * r * g_ref[...]).astype(BF16)
    o_ref[...] = n
    logits = jnp.dot(n, wr_ref[...], preferred_element_type=F32)
    lane = lax.broadcasted_iota(jnp.int32, logits.shape, 1)
    logits = jnp.where(lane < N_EXPERTS, logits, NEG_BIG)
    v1 = jnp.max(logits, axis=-1, keepdims=True)
    i1 = jnp.min(jnp.where(logits == v1, lane, LANES), axis=-1, keepdims=True)
    rest = jnp.where(lane == i1, NEG_BIG, logits)
    v2 = jnp.max(rest, axis=-1, keepdims=True)
    i2 = jnp.min(jnp.where(rest == v2, lane, LANES), axis=-1, keepdims=True)
    e2 = jnp.exp(v2 - v1)
    w1 = 1.0 / (1.0 + e2)
    w2 = e2 / (1.0 + e2)
    gates_ref[...] = jnp.where(lane == i1, w1, 0.0) + jnp.where(lane == i2, w2, 0.0)


def rmsnorm_router(x, g, w_router_pad, tm=512):
    n, d = x.shape
    return pl.pallas_call(
        _rmsnorm_router_kernel,
        out_shape=(jax.ShapeDtypeStruct((n, d), BF16),
                   jax.ShapeDtypeStruct((n, LANES), F32)),
        grid=(n // tm,),
        in_specs=[pl.BlockSpec((tm, d), lambda i: (i, 0)),
                  pl.BlockSpec((1, d), lambda i: (0, 0)),
                  pl.BlockSpec((d, LANES), lambda i: (0, 0))],
        out_specs=(pl.BlockSpec((tm, d), lambda i: (i, 0)),
                   pl.BlockSpec((tm, LANES), lambda i: (i, 0))),
        compiler_params=_params("parallel"),
        name="rmsnorm_router",
    )(x, g.reshape(1, d), w_router_pad)


def _proj_kernel(a_ref, w_ref, o_ref):
    o_ref[...] = jnp.dot(a_ref[...], w_ref[...], preferred_element_type=F32).astype(o_ref.dtype)


def in_proj(a, w, tm=MM_TILE, tn=MM_TILE):
    m, k = a.shape
    n = w.shape[1]
    return pl.pallas_call(
        _proj_kernel,
        out_shape=jax.ShapeDtypeStruct((m, n), BF16),
        grid=(m // tm, n // tn),
        in_specs=[pl.BlockSpec((tm, k), lambda i, j: (i, 0)),
                  pl.BlockSpec((k, tn), lambda i, j: (0, j))],
        out_specs=pl.BlockSpec((tm, tn), lambda i, j: (i, j)),
        compiler_params=_params("parallel", "parallel"),
        name="in_proj",
    )(a, w)


def _out_proj_kernel(a1_ref, a2_ref, w1_ref, w2_ref, res_ref, o_ref):
    acc = jnp.dot(a1_ref[...], w1_ref[...], preferred_element_type=F32)
    acc += jnp.dot(a2_ref[...], w2_ref[...], preferred_element_type=F32)
    o_ref[...] = res_ref[...] + acc


def out_proj(a1, a2, w1, w2, res, tm=MM_TILE, tn=MM_TILE):
    m, k1 = a1.shape
    k2 = a2.shape[1]
    n = w1.shape[1]
    return pl.pallas_call(
        _out_proj_kernel,
        out_shape=jax.ShapeDtypeStruct((m, n), F32),
        grid=(m // tm, n // tn),
        in_specs=[pl.BlockSpec((tm, k1), lambda i, j: (i, 0)),
                  pl.BlockSpec((tm, k2), lambda i, j: (i, 0)),
                  pl.BlockSpec((k1, tn), lambda i, j: (0, j)),
                  pl.BlockSpec((k2, tn), lambda i, j: (0, j)),
                  pl.BlockSpec((tm, tn), lambda i, j: (i, j))],
        out_specs=pl.BlockSpec((tm, tn), lambda i, j: (i, j)),
        compiler_params=_params("parallel", "parallel"),
        name="out_proj",
    )(a1, a2, w1, w2, res)


def _silu(x):
    return x * (1.0 / (1.0 + jnp.exp(-x)))


def _ffn_up_kernel(n_ref, wg_ref, wu_ref, o_ref):
    a = n_ref[...]
    g = jnp.dot(a, wg_ref[...], preferred_element_type=F32)
    u = jnp.dot(a, wu_ref[...], preferred_element_type=F32)
    o_ref[...] = (_silu(g) * u).astype(o_ref.dtype)


def ffn_up(n, wg, wu, tm=MM_TILE, tn=512):
    m, k = n.shape
    f = wg.shape[1]
    return pl.pallas_call(
        _ffn_up_kernel,
        out_shape=jax.ShapeDtypeStruct((m, f), BF16),
        grid=(m // tm, f // tn),
        in_specs=[pl.BlockSpec((tm, k), lambda i, j: (i, 0)),
                  pl.BlockSpec((k, tn), lambda i, j: (0, j)),
                  pl.BlockSpec((k, tn), lambda i, j: (0, j))],
        out_specs=pl.BlockSpec((tm, tn), lambda i, j: (i, j)),
        compiler_params=_params("parallel", "parallel"),
        name="ffn_up",
    )(n, wg, wu)


def _moe_up_kernel(n_ref, gates_ref, wg_ref, wu_ref, o_ref, *, tiles_per_expert):
    e = pl.program_id(1) // tiles_per_expert
    a = n_ref[...]
    g = jnp.dot(a, wg_ref[...], preferred_element_type=F32)
    u = jnp.dot(a, wu_ref[...], preferred_element_type=F32)
    gates = gates_ref[...]
    lane = lax.broadcasted_iota(jnp.int32, gates.shape, 1)
    gate = jnp.sum(jnp.where(lane == e, gates, 0.0), axis=-1, keepdims=True)
    o_ref[...] = (_silu(g) * u * gate).astype(o_ref.dtype)


def moe_up(n, gates, wg, wu, tm=MM_TILE, tn=512):
    m, k = n.shape
    n_e, _, f = wg.shape
    tpe = f // tn
    w_spec = pl.BlockSpec((None, k, tn), lambda i, j: (j // tpe, 0, j % tpe))
    return pl.pallas_call(
        functools.partial(_moe_up_kernel, tiles_per_expert=tpe),
        out_shape=jax.ShapeDtypeStruct((m, n_e * f), BF16),
        grid=(m // tm, n_e * tpe),
        in_specs=[pl.BlockSpec((tm, k), lambda i, j: (i, 0)),
                  pl.BlockSpec((tm, LANES), lambda i, j: (i, 0)),
                  w_spec, w_spec],
        out_specs=pl.BlockSpec((tm, tn), lambda i, j: (i, j)),
        compiler_params=_params("parallel", "parallel"),
        name="moe_up",
    )(n, gates, wg, wu)


def _ffn_down_kernel(a_ref, w_ref, res_ref, o_ref):
    @pl.when(pl.program_id(2) == 0)
    def _():
        o_ref[...] = res_ref[...]

    o_ref[...] += jnp.dot(a_ref[...], w_ref[...], preferred_element_type=F32)


def ffn_down(a, w, res, tm=MM_TILE, tn=MM_TILE, tk=4096):
    m, k = a.shape
    n = w.shape[1]
    return pl.pallas_call(
        _ffn_down_kernel,
        out_shape=jax.ShapeDtypeStruct((m, n), F32),
        grid=(m // tm, n // tn, k // tk),
        in_specs=[pl.BlockSpec((tm, tk), lambda i, j, kk: (i, kk)),
                  pl.BlockSpec((tk, tn), lambda i, j, kk: (kk, j)),
                  pl.BlockSpec((tm, tn), lambda i, j, kk: (i, j))],
        out_specs=pl.BlockSpec((tm, tn), lambda i, j, kk: (i, j)),
        compiler_params=_params("parallel", "parallel", "arbitrary"),
        name="ffn_down",
    )(a, w, res)


def _alibi_slope(h):
    return float(2.0 ** (-8.0 * (h + 1) / SWA_Q_HEADS))


def _swa_kernel(q_ref, kp_ref, kc_ref, vp_ref, vc_ref, sink_ref, g_ref, o_ref, acc_ref):
    nb = pl.program_id(1)
    qi = lax.broadcasted_iota(jnp.int32, (WINDOW, 2 * WINDOW), 0)
    kj = lax.broadcasted_iota(jnp.int32, (WINDOW, 2 * WINDOW), 1)
    dist = qi - kj + WINDOW
    valid = (dist >= 0) & (dist < WINDOW) & ((kj >= WINDOW) | (nb > 0))
    dist_f = dist.astype(F32)
    zeros_half = jnp.zeros((2 * WINDOW, SWA_HEAD_DIM), BF16)
    sinks = sink_ref[...]

    for j in range(SWA_KV_HEADS):
        cols = slice(j * SWA_HEAD_DIM, (j + 1) * SWA_HEAD_DIM)
        k_j = jnp.concatenate([kp_ref[:, cols], kc_ref[:, cols]], axis=0)
        v_j = jnp.concatenate([vp_ref[:, cols], vc_ref[:, cols]], axis=0)
        k_lo = jnp.concatenate([k_j, zeros_half], axis=1)
        k_hi = jnp.concatenate([zeros_half, k_j], axis=1)
        v_lo = jnp.concatenate([v_j, zeros_half], axis=1)
        v_hi = jnp.concatenate([zeros_half, v_j], axis=1)
        for p in range(SWA_GROUP // 2):
            pair = j * (SWA_GROUP // 2) + p
            lanes = slice(pair * LANES, (pair + 1) * LANES)
            q_pair = q_ref[:, lanes] * jnp.asarray(SWA_HEAD_DIM ** -0.5, BF16)
            out_pair = None
            for half, (k_pad, v_pad) in enumerate(((k_lo, v_lo), (k_hi, v_hi))):
                h = 2 * pair + half
                s = lax.dot_general(q_pair, k_pad, (((1,), (1,)), ((), ())),
                                    preferred_element_type=F32)
                s = jnp.where(valid, s - _alibi_slope(h) * dist_f, NEG_BIG)
                sink = sinks[:, h:h + 1]
                m = jnp.maximum(jnp.max(s, axis=-1, keepdims=True), sink)
                e = jnp.exp(s - m)
                denom = jnp.sum(e, axis=-1, keepdims=True) + jnp.exp(sink - m)
                pr = (e * (1.0 / denom)).astype(BF16)
                o = jnp.dot(pr, v_pad, preferred_element_type=F32)
                out_pair = o if out_pair is None else out_pair + o
            acc_ref[:, lanes] = out_pair

    acc = acc_ref[...]
    r = lax.rsqrt(jnp.mean(acc * acc, axis=-1, keepdims=True) + EPS)
    o_ref[...] = (acc * r * g_ref[...]).astype(o_ref.dtype)


def swa_attention(proj, sinks, norm_g, batch, seq):
    nb = seq // WINDOW
    rows = lambda b, n: b * nb + n
    prev = lambda b, n: b * nb + jnp.maximum(n - 1, 0)
    kcol = COL_SK // SWA_KV_WIDTH
    vcol = COL_SV // SWA_KV_WIDTH
    return pl.pallas_call(
        _swa_kernel,
        out_shape=jax.ShapeDtypeStruct((batch * seq, SWA_WIDTH), BF16),
        grid=(batch, nb),
        in_specs=[pl.BlockSpec((WINDOW, SWA_WIDTH), lambda b, n: (rows(b, n), COL_SQ // SWA_WIDTH)),
                  pl.BlockSpec((WINDOW, SWA_KV_WIDTH), lambda b, n: (prev(b, n), kcol)),
                  pl.BlockSpec((WINDOW, SWA_KV_WIDTH), lambda b, n: (rows(b, n), kcol)),
                  pl.BlockSpec((WINDOW, SWA_KV_WIDTH), lambda b, n: (prev(b, n), vcol)),
                  pl.BlockSpec((WINDOW, SWA_KV_WIDTH), lambda b, n: (rows(b, n), vcol)),
                  pl.BlockSpec((1, SWA_Q_HEADS), lambda b, n: (0, 0)),
                  pl.BlockSpec((1, SWA_WIDTH), lambda b, n: (0, 0))],
        out_specs=pl.BlockSpec((WINDOW, SWA_WIDTH), lambda b, n: (rows(b, n), 0)),
        scratch_shapes=[pltpu.VMEM((WINDOW, SWA_WIDTH), F32)],
        compiler_params=_params("parallel", "parallel"),
        name="swa_attention",
    )(proj, proj, proj, proj, proj, sinks.reshape(1, SWA_Q_HEADS), norm_g.reshape(1, SWA_WIDTH))


def _gla_kernel(q_ref, k_ref, v_ref, gg_ref, glr_ref, gw_ref, gb_ref, ng_ref, o_ref, st_ref):
    c = GLA_CHUNK

    @pl.when(pl.program_id(1) == 0)
    def _():
        st_ref[...] = jnp.zeros_like(st_ref)

    logits = jnp.dot(glr_ref[...], gw_ref[...], preferred_element_type=F32) + gb_ref[...]
    log_a = (jnp.minimum(logits, 0.0) - jnp.log(1.0 + jnp.exp(-jnp.abs(logits)))) * (1.0 / GLA_TAU)
    ri = lax.broadcasted_iota(jnp.int32, (c, c), 0)
    ci = lax.broadcasted_iota(jnp.int32, (c, c), 1)
    causal = ri >= ci
    tri = jnp.where(causal, 1.0, 0.0).astype(BF16)
    la_hi = log_a.astype(BF16)
    la_lo = (log_a - la_hi.astype(F32)).astype(BF16)
    b = (jnp.dot(tri, la_hi, preferred_element_type=F32)
         + jnp.dot(tri, la_lo, preferred_element_type=F32))
    b_last = b[c - 1:c, :]
    q = q_ref[...].astype(F32)
    k = k_ref[...].astype(F32)
    q_dec = (q * jnp.exp(b) * (GLA_DK ** -0.5)).astype(BF16)
    k_inv = (k * jnp.exp(-b)).astype(BF16)
    k_end = (k * jnp.exp(b_last - b)).astype(BF16)
    dec = jnp.exp(b_last)

    for h in range(GLA_HEADS):
        kc = slice(h * GLA_DK, (h + 1) * GLA_DK)
        vc = slice(h * GLA_DV, (h + 1) * GLA_DV)
        v = v_ref[:, vc]
        a = lax.dot_general(q_dec[:, kc], k_inv[:, kc], (((1,), (1,)), ((), ())),
                            preferred_element_type=F32)
        a = jnp.where(causal, a, 0.0).astype(BF16)
        st = st_ref[h]
        o = jnp.dot(a, v, preferred_element_type=F32)
        o += lax.dot_general(q_dec[:, kc], st.astype(BF16), (((1,), (1,)), ((), ())),
                             preferred_element_type=F32)
        upd = lax.dot_general(v, k_end[:, kc], (((0,), (0,)), ((), ())),
                              preferred_element_type=F32)
        st_ref[h] = st * dec[:, kc] + upd
        r = lax.rsqrt(jnp.mean(o * o, axis=-1, keepdims=True) + EPS)
        gate = _silu(gg_ref[:, vc].astype(F32))
        o_ref[:, vc] = (o * r * ng_ref[...] * gate).astype(o_ref.dtype)


def gla_attention(proj, gate_w_pad, gate_b, norm_g, batch, seq):
    c = GLA_CHUNK
    nc = seq // c
    rows = lambda b, n: b * nc + n
    return pl.pallas_call(
        _gla_kernel,
        out_shape=jax.ShapeDtypeStruct((batch * seq, GLA_WIDTH), BF16),
        grid=(batch, nc),
        in_specs=[pl.BlockSpec((c, GLA_KEY_WIDTH), lambda b, n: (rows(b, n), COL_GQ // GLA_KEY_WIDTH)),
                  pl.BlockSpec((c, GLA_KEY_WIDTH), lambda b, n: (rows(b, n), COL_GK // GLA_KEY_WIDTH)),
                  pl.BlockSpec((c, GLA_WIDTH), lambda b, n: (rows(b, n), COL_GV // GLA_WIDTH)),
                  pl.BlockSpec((c, GLA_WIDTH), lambda b, n: (rows(b, n), COL_GG // GLA_WIDTH)),
                  pl.BlockSpec((c, LANES), lambda b, n: (rows(b, n), COL_GLR // LANES)),
                  pl.BlockSpec((LANES, GLA_KEY_WIDTH), lambda b, n: (0, 0)),
                  pl.BlockSpec((1, GLA_KEY_WIDTH), lambda b, n: (0, 0)),
                  pl.BlockSpec((1, GLA_DV), lambda b, n: (0, 0))],
        out_specs=pl.BlockSpec((c, GLA_WIDTH), lambda b, n: (rows(b, n), 0)),
        scratch_shapes=[pltpu.VMEM((GLA_HEADS, GLA_DV, GLA_DK), F32)],
        compiler_params=_params("parallel", "arbitrary"),
        name="gla_attention",
    )(proj, proj, proj, proj, proj, gate_w_pad, gate_b.reshape(1, GLA_KEY_WIDTH),
      norm_g.reshape(1, GLA_DV))


def _prep_w_in(w):
    sq, sk, sv, gq, gk, gv, gg, glr = jnp.split(
        w, [2048, 2304, 2560, 3584, 4608, 6656, 8704], axis=-1)
    pad = jnp.zeros((w.shape[0], PROJ_COLS - COL_GLR - GLA_GATE_RANK), w.dtype)
    return jnp.concatenate([sq, gv, gg, gq, gk, sk, sv, glr, pad], axis=-1).astype(BF16)


def kernel(x, attn_norm_g, w_in, gla_gate_w, gla_gate_b, gla_norm_g, swa_sinks, swa_norm_g, w_out,
           ffn_norm_g, dense_w_gate, dense_w_up, dense_w_down, moe_router, moe_w_gate, moe_w_up,
           moe_w_down, final_norm_g):
    batch, seq, d = x.shape
    xs = x.reshape(batch * seq, d)
    for layer in range(DEPTH):
        n = rmsnorm(xs, attn_norm_g[layer], BF16)
        proj = in_proj(n, _prep_w_in(w_in[layer]))
        swa = swa_attention(proj, swa_sinks[layer], swa_norm_g[layer], batch, seq)
        gate_w_pad = jnp.pad(gla_gate_w[layer], ((0, LANES - GLA_GATE_RANK), (0, 0))).astype(BF16)
        gla = gla_attention(proj, gate_w_pad, gla_gate_b[layer], gla_norm_g[layer], batch, seq)
        w_o = w_out[layer].astype(BF16)
        h = out_proj(swa, gla, w_o[:SWA_WIDTH], w_o[SWA_WIDTH:], xs)
        j = layer // 2
        if layer % 2 == 0:
            n2 = rmsnorm(h, ffn_norm_g[layer], BF16)
            act = ffn_up(n2, dense_w_gate[j].astype(BF16), dense_w_up[j].astype(BF16))
            xs = ffn_down(act, dense_w_down[j].astype(BF16), h)
        else:
            wr = jnp.pad(moe_router[j], ((0, 0), (0, LANES - N_EXPERTS))).astype(BF16)
            n2, gates = rmsnorm_router(h, ffn_norm_g[layer], wr)
            act = moe_up(n2, gates, moe_w_gate[j].astype(BF16), moe_w_up[j].astype(BF16))
            wd = moe_w_down[j].astype(BF16).reshape(N_EXPERTS * D_FF_EXPERT, d)
            xs = ffn_down(act, wd, h)
    out = rmsnorm(xs, final_norm_g, F32)
    return out.reshape(batch, seq, d)
```

```python
import functools

import jax
import jax.numpy as jnp
import numpy as np
from jax import lax
from jax.experimental import pallas as pl
from jax.experimental.pallas import tpu as pltpu

BF16 = jnp.bfloat16
F32 = jnp.float32

D_MODEL = 4096
DEPTH = 4
EPS = 1e-5

SWA_WIDTH = 2048
SWA_HEAD_DIM = 64
SWA_Q_HEADS = 32
SWA_KV_HEADS = 4
SWA_GROUP = 8
SWA_KV_WIDTH = SWA_KV_HEADS * SWA_HEAD_DIM
WINDOW = 128

GLA_WIDTH = 2048
GLA_HEADS = 4
GLA_KEY_WIDTH = 1024
GLA_DK = 256
GLA_DV = 512
GLA_GATE_RANK = 16
GLA_TAU = 16.0
GLA_CHUNK = 64

D_FF_DENSE = 8192
N_EXPERTS = 8
D_FF_EXPERT = 2048

LANES = 128
VMEM_LIMIT = 56 * 1024 * 1024

COL_SQ = 0
COL_GV = 2048
COL_GG = 4096
COL_GQ = 6144
COL_GK = 7168
COL_SK = 8192
COL_SV = 8448
COL_GLR = 8704
PROJ_COLS = 9216

MM_TILE = 1024
NEG_BIG = -1e30


def _params(*sem):
    return pltpu.CompilerParams(dimension_semantics=sem, vmem_limit_bytes=VMEM_LIMIT)


def _rmsnorm_kernel(x_ref, g_ref, o_ref):
    x = x_ref[...]
    r = lax.rsqrt(jnp.mean(x * x, axis=-1, keepdims=True) + EPS)
    o_ref[...] = (x * r * g_ref[...]).astype(o_ref.dtype)


def rmsnorm(x, g, out_dtype, tm=512):
    n, d = x.shape
    return pl.pallas_call(
        _rmsnorm_kernel,
        out_shape=jax.ShapeDtypeStruct((n, d), out_dtype),
        grid=(n // tm,),
        in_specs=[pl.BlockSpec((tm, d), lambda i: (i, 0)),
                  pl.BlockSpec((1, d), lambda i: (0, 0))],
        out_specs=pl.BlockSpec((tm, d), lambda i: (i, 0)),
        compiler_params=_params("parallel"),
        name="rmsnorm",
    )(x, g.reshape(1, d))


def _rmsnorm_router_kernel(x_ref, g_ref, wr_ref, o_ref, idx_ref, wts_ref):
    x = x_ref[...]
    r = lax.rsqrt(jnp.mean(x * x, axis=-1, keepdims=True) + EPS)
    n = x * r * g_ref[...]
    o_ref[...] = n
    logits = jnp.dot(n.astype(BF16), wr_ref[...], preferred_element_type=F32)
    lane = lax.broadcasted_iota(jnp.int32, logits.shape, 1)
    logits = jnp.where(lane < N_EXPERTS, logits, NEG_BIG)
    v1 = jnp.max(logits, axis=-1, keepdims=True)
    i1 = jnp.min(jnp.where(logits == v1, lane, LANES), axis=-1, keepdims=True)
    rest = jnp.where(lane == i1, NEG_BIG, logits)
    v2 = jnp.max(rest, axis=-1, keepdims=True)
    i2 = jnp.min(jnp.where(rest == v2, lane, LANES), axis=-1, keepdims=True)
    e2 = jnp.exp(v2 - v1)
    w1 = 1.0 / (1.0 + e2)
    w2 = e2 / (1.0 + e2)
    idx_ref[...] = jnp.where(lane == 0, i1, i2)
    wts_ref[...] = jnp.where(lane == 0, w1, w2)


def rmsnorm_router(x, g, w_router_pad, tm=512):
    n, d = x.shape
    return pl.pallas_call(
        _rmsnorm_router_kernel,
        out_shape=(jax.ShapeDtypeStruct((n, d), F32),
                   jax.ShapeDtypeStruct((n, LANES), jnp.int32),
                   jax.ShapeDtypeStruct((n, LANES), F32)),
        grid=(n // tm,),
        in_specs=[pl.BlockSpec((tm, d), lambda i: (i, 0)),
                  pl.BlockSpec((1, d), lambda i: (0, 0)),
                  pl.BlockSpec((d, LANES), lambda i: (0, 0))],
        out_specs=(pl.BlockSpec((tm, d), lambda i: (i, 0)),
                   pl.BlockSpec((tm, LANES), lambda i: (i, 0)),
                   pl.BlockSpec((tm, LANES), lambda i: (i, 0))),
        compiler_params=_params("parallel"),
        name="rmsnorm_router",
    )(x, g.reshape(1, d), w_router_pad)


def _proj_kernel(a_ref, w_ref, o_ref):
    o_ref[...] = jnp.dot(a_ref[...], w_ref[...], preferred_element_type=F32).astype(o_ref.dtype)


def in_proj(a, w, tm=MM_TILE, tn=MM_TILE):
    m, k = a.shape
    n = w.shape[1]
    return pl.pallas_call(
        _proj_kernel,
        out_shape=jax.ShapeDtypeStruct((m, n), BF16),
        grid=(m // tm, n // tn),
        in_specs=[pl.BlockSpec((tm, k), lambda i, j: (i, 0)),
                  pl.BlockSpec((k, tn), lambda i, j: (0, j))],
        out_specs=pl.BlockSpec((tm, tn), lambda i, j: (i, j)),
        compiler_params=_params("parallel", "parallel"),
        name="in_proj",
    )(a, w)


def _out_proj_kernel(a1_ref, a2_ref, w1_ref, w2_ref, res_ref, o_ref):
    acc = jnp.dot(a1_ref[...], w1_ref[...], preferred_element_type=F32)
    acc += jnp.dot(a2_ref[...], w2_ref[...], preferred_element_type=F32)
    o_ref[...] = res_ref[...] + acc


def out_proj(a1, a2, w1, w2, res, tm=MM_TILE, tn=MM_TILE):
    m, k1 = a1.shape
    k2 = a2.shape[1]
    n = w1.shape[1]
    return pl.pallas_call(
        _out_proj_kernel,
        out_shape=jax.ShapeDtypeStruct((m, n), F32),
        grid=(m // tm, n // tn),
        in_specs=[pl.BlockSpec((tm, k1), lambda i, j: (i, 0)),
                  pl.BlockSpec((tm, k2), lambda i, j: (i, 0)),
                  pl.BlockSpec((k1, tn), lambda i, j: (0, j)),
                  pl.BlockSpec((k2, tn), lambda i, j: (0, j)),
                  pl.BlockSpec((tm, tn), lambda i, j: (i, j))],
        out_specs=pl.BlockSpec((tm, tn), lambda i, j: (i, j)),
        compiler_params=_params("parallel", "parallel"),
        name="out_proj",
    )(a1, a2, w1, w2, res)


def _silu(x):
    return x * (1.0 / (1.0 + jnp.exp(-x)))


def _ffn_up_kernel(n_ref, wg_ref, wu_ref, o_ref):
    a = n_ref[...]
    g = jnp.dot(a, wg_ref[...], preferred_element_type=F32)
    u = jnp.dot(a, wu_ref[...], preferred_element_type=F32)
    o_ref[...] = (_silu(g) * u).astype(o_ref.dtype)


def ffn_up(n, wg, wu, tm=MM_TILE, tn=512):
    m, k = n.shape
    f = wg.shape[1]
    return pl.pallas_call(
        _ffn_up_kernel,
        out_shape=jax.ShapeDtypeStruct((m, f), BF16),
        grid=(m // tm, f // tn),
        in_specs=[pl.BlockSpec((tm, k), lambda i, j: (i, 0)),
                  pl.BlockSpec((k, tn), lambda i, j: (0, j)),
                  pl.BlockSpec((k, tn), lambda i, j: (0, j))],
        out_specs=pl.BlockSpec((tm, tn), lambda i, j: (i, j)),
        compiler_params=_params("parallel", "parallel"),
        name="ffn_up",
    )(n, wg, wu)


MOE_TILE = 256
GATHER_ROWS = 512


def _route_plan(idx, n_tiles):
    n = idx.shape[0]
    ea = jnp.concatenate([idx[:, 0], idx[:, 1]])
    onehot = (ea[:, None] == jnp.arange(N_EXPERTS, dtype=jnp.int32)[None, :]).astype(jnp.int32)
    incl = jnp.cumsum(onehot, axis=0)
    counts = incl[-1]
    padded = ((counts + MOE_TILE - 1) // MOE_TILE) * MOE_TILE
    ends = jnp.cumsum(padded)
    starts = ends - padded
    pos = jnp.sum(onehot * (starts[None, :] + incl - 1), axis=1)
    token = jnp.arange(2 * n, dtype=jnp.int32) % n
    row_token = jnp.zeros((n_tiles * MOE_TILE,), jnp.int32).at[pos].set(token)
    tile_start = jnp.arange(n_tiles, dtype=jnp.int32) * MOE_TILE
    tile_expert = jnp.minimum(jnp.sum((tile_start[:, None] >= ends[None, :]).astype(jnp.int32), axis=1),
                              N_EXPERTS - 1)
    n_valid = (ends[-1:] // MOE_TILE).astype(jnp.int32)
    return row_token, pos.astype(jnp.int32), tile_expert.astype(jnp.int32), n_valid


def _gather_rows_kernel(idx_ref, src_ref, dst_ref, sem):
    base = pl.program_id(0) * GATHER_ROWS

    def row_copy(i):
        return pltpu.make_async_copy(src_ref.at[pl.ds(idx_ref[0, 0, i], 1)],
                                     dst_ref.at[pl.ds(base + i, 1)], sem)

    def start(i, carry):
        row_copy(i).start()
        return carry

    def wait(i, carry):
        row_copy(i).wait()
        return carry

    lax.fori_loop(0, GATHER_ROWS, start, 0, unroll=8)
    lax.fori_loop(0, GATHER_ROWS, wait, 0, unroll=8)


def gather_rows(src, idx):
    r = idx.shape[0]
    steps = r // GATHER_ROWS
    return pl.pallas_call(
        _gather_rows_kernel,
        out_shape=jax.ShapeDtypeStruct((r, src.shape[1]), src.dtype),
        grid=(steps,),
        in_specs=[pl.BlockSpec((1, 1, GATHER_ROWS), lambda i: (i, 0, 0), memory_space=pltpu.SMEM),
                  pl.BlockSpec(memory_space=pl.ANY)],
        out_specs=pl.BlockSpec(memory_space=pl.ANY),
        scratch_shapes=[pltpu.SemaphoreType.DMA],
        compiler_params=_params("arbitrary"),
        name="gather_rows",
    )(idx.reshape(steps, 1, GATHER_ROWS), src)


def _moe_up_kernel(te_ref, nv_ref, x_ref, wg_ref, wu_ref, o_ref):
    valid = pl.program_id(1) < nv_ref[0]

    @pl.when(valid)
    def _():
        a = x_ref[...].astype(BF16)
        g = jnp.dot(a, wg_ref[...], preferred_element_type=F32)
        u = jnp.dot(a, wu_ref[...], preferred_element_type=F32)
        o_ref[...] = (_silu(g) * u).astype(o_ref.dtype)

    @pl.when(jnp.logical_not(valid))
    def _():
        o_ref[...] = jnp.zeros_like(o_ref)


def moe_up(x_rows, tile_expert, n_valid, wg, wu, tn=1024):
    r, k = x_rows.shape
    f = wg.shape[2]
    tm = MOE_TILE
    w_spec = pl.BlockSpec((None, k, tn), lambda j, m, te, nv: (te[m], 0, j))
    return pl.pallas_call(
        _moe_up_kernel,
        out_shape=jax.ShapeDtypeStruct((r, f), BF16),
        grid_spec=pltpu.PrefetchScalarGridSpec(
            num_scalar_prefetch=2,
            grid=(f // tn, r // tm),
            in_specs=[pl.BlockSpec((tm, k), lambda j, m, te, nv: (m, 0)), w_spec, w_spec],
            out_specs=pl.BlockSpec((tm, tn), lambda j, m, te, nv: (m, j))),
        compiler_params=_params("parallel", "arbitrary"),
        name="moe_up",
    )(tile_expert, n_valid, x_rows, wg, wu)


def _moe_down_kernel(te_ref, nv_ref, a_ref, w_ref, o_ref):
    valid = pl.program_id(0) < nv_ref[0]

    @pl.when(valid)
    def _():
        o_ref[...] = jnp.dot(a_ref[...], w_ref[...], preferred_element_type=F32)

    @pl.when(jnp.logical_not(valid))
    def _():
        o_ref[...] = jnp.zeros_like(o_ref)


def moe_down(act_rows, tile_expert, n_valid, wd):
    r, f = act_rows.shape
    d = wd.shape[2]
    tm = MOE_TILE
    return pl.pallas_call(
        _moe_down_kernel,
        out_shape=jax.ShapeDtypeStruct((r, d), F32),
        grid_spec=pltpu.PrefetchScalarGridSpec(
            num_scalar_prefetch=2,
            grid=(r // tm,),
            in_specs=[pl.BlockSpec((tm, f), lambda m, te, nv: (m, 0)),
                      pl.BlockSpec((None, f, d), lambda m, te, nv: (te[m], 0, 0))],
            out_specs=pl.BlockSpec((tm, d), lambda m, te, nv: (m, 0))),
        compiler_params=_params("arbitrary"),
        name="moe_down",
    )(tile_expert, n_valid, act_rows, wd)


def _moe_combine_kernel(h_ref, y1_ref, y2_ref, wts_ref, o_ref):
    w = wts_ref[...]
    o_ref[...] = h_ref[...] + w[:, 0:1] * y1_ref[...] + w[:, 1:2] * y2_ref[...]


def moe_combine(h, y_pairs, wts, tm=256):
    n, d = h.shape
    nt = n // tm
    return pl.pallas_call(
        _moe_combine_kernel,
        out_shape=jax.ShapeDtypeStruct((n, d), F32),
        grid=(nt,),
        in_specs=[pl.BlockSpec((tm, d), lambda i: (i, 0)),
                  pl.BlockSpec((tm, d), lambda i: (i, 0)),
                  pl.BlockSpec((tm, d), lambda i: (i + nt, 0)),
                  pl.BlockSpec((tm, LANES), lambda i: (i, 0))],
        out_specs=pl.BlockSpec((tm, d), lambda i: (i, 0)),
        compiler_params=_params("parallel"),
        name="moe_combine",
    )(h, y_pairs, y_pairs, wts)


def moe_ffn(h, norm_g, w_router_pad, wg, wu, wd):
    n = h.shape[0]
    n_tiles = (2 * n) // MOE_TILE + N_EXPERTS
    n2, idx, wts = rmsnorm_router(h, norm_g, w_router_pad)
    row_token, pos, tile_expert, n_valid = _route_plan(idx, n_tiles)
    x_rows = gather_rows(n2, row_token)
    act_rows = moe_up(x_rows, tile_expert, n_valid, wg, wu)
    y_rows = moe_down(act_rows, tile_expert, n_valid, wd)
    y_pairs = gather_rows(y_rows, pos)
    return moe_combine(h, y_pairs, wts)


def _ffn_down_kernel(a_ref, w_ref, res_ref, o_ref):
    @pl.when(pl.program_id(2) == 0)
    def _():
        o_ref[...] = res_ref[...]

    o_ref[...] += jnp.dot(a_ref[...], w_ref[...], preferred_element_type=F32)


def ffn_down(a, w, res, tm=MM_TILE, tn=MM_TILE, tk=4096):
    m, k = a.shape
    n = w.shape[1]
    return pl.pallas_call(
        _ffn_down_kernel,
        out_shape=jax.ShapeDtypeStruct((m, n), F32),
        grid=(m // tm, n // tn, k // tk),
        in_specs=[pl.BlockSpec((tm, tk), lambda i, j, kk: (i, kk)),
                  pl.BlockSpec((tk, tn), lambda i, j, kk: (kk, j)),
                  pl.BlockSpec((tm, tn), lambda i, j, kk: (i, j))],
        out_specs=pl.BlockSpec((tm, tn), lambda i, j, kk: (i, j)),
        compiler_params=_params("parallel", "parallel", "arbitrary"),
        name="ffn_down",
    )(a, w, res)


def _alibi_slope(h):
    return float(2.0 ** (-8.0 * (h + 1) / SWA_Q_HEADS))


def _swa_bias_table():
    qi = np.arange(WINDOW)[:, None]
    kj = np.arange(2 * WINDOW)[None, :]
    dist = qi - kj + WINDOW
    inside = (dist >= 0) & (dist < WINDOW)
    slopes = np.array([_alibi_slope(h) for h in range(SWA_Q_HEADS)], np.float64)
    bias = -slopes[:, None, None] * dist[None].astype(np.float64)
    later = np.where(inside[None], bias, NEG_BIG)
    first = np.where((inside & (kj >= WINDOW))[None], bias, NEG_BIG)
    table = np.stack([first, later]).astype(np.float32)
    return table.reshape(2, SWA_KV_HEADS, SWA_GROUP * WINDOW, 2 * WINDOW)


def _swa_kernel(q_ref, kp_ref, kc_ref, vp_ref, vc_ref, bias_ref, sink_ref, g_ref, o_ref, acc_ref):
    nt = (((1,), (1,)), ((), ()))
    zeros_half = jnp.zeros((2 * WINDOW, SWA_HEAD_DIM), BF16)
    ones = jnp.ones((2 * WINDOW, LANES), BF16)
    pairs = SWA_GROUP // 2

    for j in range(SWA_KV_HEADS):
        cols = slice(j * SWA_HEAD_DIM, (j + 1) * SWA_HEAD_DIM)
        k_j = jnp.concatenate([kp_ref[:, cols], kc_ref[:, cols]], axis=0)
        v_j = jnp.concatenate([vp_ref[:, cols], vc_ref[:, cols]], axis=0)
        k_pads = (jnp.concatenate([k_j, zeros_half], axis=1), jnp.concatenate([zeros_half, k_j], axis=1))
        v_pads = (jnp.concatenate([v_j, zeros_half], axis=1), jnp.concatenate([zeros_half, v_j], axis=1))
        parts = []
        for p in range(pairs):
            lanes = slice((j * pairs + p) * LANES, (j * pairs + p + 1) * LANES)
            q_pair = q_ref[:, lanes] * jnp.asarray(SWA_HEAD_DIM ** -0.5, BF16)
            for k_pad in k_pads:
                parts.append(lax.dot_general(q_pair, k_pad, nt, preferred_element_type=F32))
        s = jnp.concatenate(parts, axis=0) + bias_ref[j]
        sink = sink_ref[j]
        m = jnp.maximum(jnp.max(s, axis=-1, keepdims=True), sink)
        e = jnp.exp(s - m).astype(BF16)
        denom = jnp.dot(e, ones, preferred_element_type=F32) + jnp.exp(sink - m)
        inv = 1.0 / denom
        for p in range(pairs):
            lanes = slice((j * pairs + p) * LANES, (j * pairs + p + 1) * LANES)
            out_pair = None
            for half in range(2):
                rows = slice((2 * p + half) * WINDOW, (2 * p + half + 1) * WINDOW)
                o = jnp.dot(e[rows], v_pads[half], preferred_element_type=F32) * inv[rows]
                out_pair = o if out_pair is None else out_pair + o
            acc_ref[:, lanes] = out_pair

    acc = acc_ref[...]
    r = lax.rsqrt(jnp.mean(acc * acc, axis=-1, keepdims=True) + EPS)
    o_ref[...] = (acc * r * g_ref[...]).astype(o_ref.dtype)


def swa_attention(proj, sinks, norm_g, batch, seq):
    nb = seq // WINDOW
    rows = lambda b, n: b * nb + n
    prev = lambda b, n: b * nb + jnp.maximum(n - 1, 0)
    kcol = COL_SK // SWA_KV_WIDTH
    vcol = COL_SV // SWA_KV_WIDTH
    gq = SWA_GROUP * WINDOW
    bias = jnp.asarray(_swa_bias_table())
    sink_rows = jnp.repeat(sinks.astype(F32), WINDOW).reshape(SWA_KV_HEADS, gq, 1)
    return pl.pallas_call(
        _swa_kernel,
        out_shape=jax.ShapeDtypeStruct((batch * seq, SWA_WIDTH), BF16),
        grid=(batch, nb),
        in_specs=[pl.BlockSpec((WINDOW, SWA_WIDTH), lambda b, n: (rows(b, n), COL_SQ // SWA_WIDTH)),
                  pl.BlockSpec((WINDOW, SWA_KV_WIDTH), lambda b, n: (prev(b, n), kcol)),
                  pl.BlockSpec((WINDOW, SWA_KV_WIDTH), lambda b, n: (rows(b, n), kcol)),
                  pl.BlockSpec((WINDOW, SWA_KV_WIDTH), lambda b, n: (prev(b, n), vcol)),
                  pl.BlockSpec((WINDOW, SWA_KV_WIDTH), lambda b, n: (rows(b, n), vcol)),
                  pl.BlockSpec((None, SWA_KV_HEADS, gq, 2 * WINDOW),
                               lambda b, n: (jnp.minimum(n, 1), 0, 0, 0)),
                  pl.BlockSpec((SWA_KV_HEADS, gq, 1), lambda b, n: (0, 0, 0)),
                  pl.BlockSpec((1, SWA_WIDTH), lambda b, n: (0, 0))],
        out_specs=pl.BlockSpec((WINDOW, SWA_WIDTH), lambda b, n: (rows(b, n), 0)),
        scratch_shapes=[pltpu.VMEM((WINDOW, SWA_WIDTH), F32)],
        compiler_params=_params("parallel", "parallel"),
        name="swa_attention",
    )(proj, proj, proj, proj, proj, bias, sink_rows, norm_g.reshape(1, SWA_WIDTH))


def _gla_kernel(q_ref, k_ref, v_ref, gg_ref, glr_ref, gw_ref, gb_ref, ng_ref, o_ref, st_ref):
    c = GLA_CHUNK

    @pl.when(pl.program_id(1) == 0)
    def _():
        st_ref[...] = jnp.zeros_like(st_ref)

    logits = jnp.dot(glr_ref[...], gw_ref[...], preferred_element_type=F32) + gb_ref[...]
    log_a = (jnp.minimum(logits, 0.0) - jnp.log(1.0 + jnp.exp(-jnp.abs(logits)))) * (1.0 / GLA_TAU)
    ri = lax.broadcasted_iota(jnp.int32, (c, c), 0)
    ci = lax.broadcasted_iota(jnp.int32, (c, c), 1)
    causal = ri >= ci
    tri = jnp.where(causal, 1.0, 0.0).astype(BF16)
    la_hi = log_a.astype(BF16)
    la_lo = (log_a - la_hi.astype(F32)).astype(BF16)
    b = (jnp.dot(tri, la_hi, preferred_element_type=F32)
         + jnp.dot(tri, la_lo, preferred_element_type=F32))
    b_last = b[c - 1:c, :]
    q = q_ref[...].astype(F32)
    k = k_ref[...].astype(F32)
    q_dec = (q * jnp.exp(b) * (GLA_DK ** -0.5)).astype(BF16)
    k_inv = (k * jnp.exp(-b)).astype(BF16)
    k_end = (k * jnp.exp(b_last - b)).astype(BF16)
    dec = jnp.exp(b_last)

    for h in range(GLA_HEADS):
        kc = slice(h * GLA_DK, (h + 1) * GLA_DK)
        vc = slice(h * GLA_DV, (h + 1) * GLA_DV)
        v = v_ref[:, vc]
        a = lax.dot_general(q_dec[:, kc], k_inv[:, kc], (((1,), (1,)), ((), ())),
                            preferred_element_type=F32)
        a = jnp.where(causal, a, 0.0).astype(BF16)
        st = st_ref[h]
        o = jnp.dot(a, v, preferred_element_type=F32)
        o += lax.dot_general(q_dec[:, kc], st.astype(BF16), (((1,), (1,)), ((), ())),
                             preferred_element_type=F32)
        upd = lax.dot_general(v, k_end[:, kc], (((0,), (0,)), ((), ())),
                              preferred_element_type=F32)
        st_ref[h] = st * dec[:, kc] + upd
        r = lax.rsqrt(jnp.mean(o * o, axis=-1, keepdims=True) + EPS)
        gate = _silu(gg_ref[:, vc].astype(F32))
        o_ref[:, vc] = (o * r * ng_ref[...] * gate).astype(o_ref.dtype)


def gla_attention(proj, gate_w_pad, gate_b, norm_g, batch, seq):
    c = GLA_CHUNK
    nc = seq // c
    rows = lambda b, n: b * nc + n
    return pl.pallas_call(
        _gla_kernel,
        out_shape=jax.ShapeDtypeStruct((batch * seq, GLA_WIDTH), BF16),
        grid=(batch, nc),
        in_specs=[pl.BlockSpec((c, GLA_KEY_WIDTH), lambda b, n: (rows(b, n), COL_GQ // GLA_KEY_WIDTH)),
                  pl.BlockSpec((c, GLA_KEY_WIDTH), lambda b, n: (rows(b, n), COL_GK // GLA_KEY_WIDTH)),
                  pl.BlockSpec((c, GLA_WIDTH), lambda b, n: (rows(b, n), COL_GV // GLA_WIDTH)),
                  pl.BlockSpec((c, GLA_WIDTH), lambda b, n: (rows(b, n), COL_GG // GLA_WIDTH)),
                  pl.BlockSpec((c, LANES), lambda b, n: (rows(b, n), COL_GLR // LANES)),
                  pl.BlockSpec((LANES, GLA_KEY_WIDTH), lambda b, n: (0, 0)),
                  pl.BlockSpec((1, GLA_KEY_WIDTH), lambda b, n: (0, 0)),
                  pl.BlockSpec((1, GLA_DV), lambda b, n: (0, 0))],
        out_specs=pl.BlockSpec((c, GLA_WIDTH), lambda b, n: (rows(b, n), 0)),
        scratch_shapes=[pltpu.VMEM((GLA_HEADS, GLA_DV, GLA_DK), F32)],
        compiler_params=_params("parallel", "arbitrary"),
        name="gla_attention",
    )(proj, proj, proj, proj, proj, gate_w_pad, gate_b.reshape(1, GLA_KEY_WIDTH),
      norm_g.reshape(1, GLA_DV))


def _prep_w_in(w):
    sq, sk, sv, gq, gk, gv, gg, glr = jnp.split(
        w, [2048, 2304, 2560, 3584, 4608, 6656, 8704], axis=-1)
    pad = jnp.zeros((w.shape[0], PROJ_COLS - COL_GLR - GLA_GATE_RANK), w.dtype)
    return jnp.concatenate([sq, gv, gg, gq, gk, sk, sv, glr, pad], axis=-1).astype(BF16)


def kernel(x, attn_norm_g, w_in, gla_gate_w, gla_gate_b, gla_norm_g, swa_sinks, swa_norm_g, w_out,
           ffn_norm_g, dense_w_gate, dense_w_up, dense_w_down, moe_router, moe_w_gate, moe_w_up,
           moe_w_down, final_norm_g):
    batch, seq, d = x.shape
    xs = x.reshape(batch * seq, d)
    for layer in range(DEPTH):
        n = rmsnorm(xs, attn_norm_g[layer], BF16)
        proj = in_proj(n, _prep_w_in(w_in[layer]))
        swa = swa_attention(proj, swa_sinks[layer], swa_norm_g[layer], batch, seq)
        gate_w_pad = jnp.pad(gla_gate_w[layer], ((0, LANES - GLA_GATE_RANK), (0, 0))).astype(BF16)
        gla = gla_attention(proj, gate_w_pad, gla_gate_b[layer], gla_norm_g[layer], batch, seq)
        w_o = w_out[layer].astype(BF16)
        h = out_proj(swa, gla, w_o[:SWA_WIDTH], w_o[SWA_WIDTH:], xs)
        j = layer // 2
        if layer % 2 == 0:
            n2 = rmsnorm(h, ffn_norm_g[layer], BF16)
            act = ffn_up(n2, dense_w_gate[j].astype(BF16), dense_w_up[j].astype(BF16))
            xs = ffn_down(act, dense_w_down[j].astype(BF16), h)
        else:
            wr = jnp.pad(moe_router[j], ((0, 0), (0, LANES - N_EXPERTS))).astype(BF16)
            xs = moe_ffn(h, ffn_norm_g[layer], wr, moe_w_gate[j].astype(BF16), moe_w_up[j].astype(BF16),
                         moe_w_down[j].astype(BF16))
    out = rmsnorm(xs, final_norm_g, F32)
    return out.reshape(batch, seq, d)
```

```python
import functools

import jax
import jax.numpy as jnp
import numpy as np
from jax import lax
from jax.experimental import pallas as pl
from jax.experimental.pallas import tpu as pltpu

BF16 = jnp.bfloat16
F32 = jnp.float32

D_MODEL = 4096
DEPTH = 4
EPS = 1e-5

SWA_WIDTH = 2048
SWA_HEAD_DIM = 64
SWA_Q_HEADS = 32
SWA_KV_HEADS = 4
SWA_GROUP = 8
SWA_KV_WIDTH = SWA_KV_HEADS * SWA_HEAD_DIM
WINDOW = 128

GLA_WIDTH = 2048
GLA_HEADS = 4
GLA_KEY_WIDTH = 1024
GLA_DK = 256
GLA_DV = 512
GLA_GATE_RANK = 16
GLA_TAU = 16.0
GLA_CHUNK = 64

D_FF_DENSE = 8192
N_EXPERTS = 8
D_FF_EXPERT = 2048

LANES = 128
VMEM_LIMIT = 56 * 1024 * 1024

COL_SQ = 0
COL_GV = 2048
COL_GG = 4096
COL_GQ = 6144
COL_GK = 7168
COL_SK = 8192
COL_SV = 8448
COL_GLR = 8704
PROJ_COLS = 9216

MM_TILE = 1024
NEG_BIG = -1e30


def _params(*sem):
    return pltpu.CompilerParams(dimension_semantics=sem, vmem_limit_bytes=VMEM_LIMIT)


def _rmsnorm_kernel(x_ref, g_ref, o_ref):
    x = x_ref[...]
    r = lax.rsqrt(jnp.mean(x * x, axis=-1, keepdims=True) + EPS)
    o_ref[...] = (x * r * g_ref[...]).astype(o_ref.dtype)


def rmsnorm(x, g, out_dtype, tm=512):
    n, d = x.shape
    return pl.pallas_call(
        _rmsnorm_kernel,
        out_shape=jax.ShapeDtypeStruct((n, d), out_dtype),
        grid=(n // tm,),
        in_specs=[pl.BlockSpec((tm, d), lambda i: (i, 0)),
                  pl.BlockSpec((1, d), lambda i: (0, 0))],
        out_specs=pl.BlockSpec((tm, d), lambda i: (i, 0)),
        compiler_params=_params("parallel"),
        name="rmsnorm",
    )(x, g.reshape(1, d))


def _rmsnorm_router_kernel(x_ref, g_ref, wr_ref, o_ref, idx_ref, wts_ref):
    x = x_ref[...]
    r = lax.rsqrt(jnp.mean(x * x, axis=-1, keepdims=True) + EPS)
    n = x * r * g_ref[...]
    o_ref[...] = n
    logits = jnp.dot(n.astype(BF16), wr_ref[...], preferred_element_type=F32)
    lane = lax.broadcasted_iota(jnp.int32, logits.shape, 1)
    logits = jnp.where(lane < N_EXPERTS, logits, NEG_BIG)
    v1 = jnp.max(logits, axis=-1, keepdims=True)
    i1 = jnp.min(jnp.where(logits == v1, lane, LANES), axis=-1, keepdims=True)
    rest = jnp.where(lane == i1, NEG_BIG, logits)
    v2 = jnp.max(rest, axis=-1, keepdims=True)
    i2 = jnp.min(jnp.where(rest == v2, lane, LANES), axis=-1, keepdims=True)
    e2 = jnp.exp(v2 - v1)
    w1 = 1.0 / (1.0 + e2)
    w2 = e2 / (1.0 + e2)
    idx_ref[...] = jnp.where(lane == 0, i1, i2)
    wts_ref[...] = jnp.where(lane == 0, w1, w2)


def rmsnorm_router(x, g, w_router_pad, tm=512):
    n, d = x.shape
    return pl.pallas_call(
        _rmsnorm_router_kernel,
        out_shape=(jax.ShapeDtypeStruct((n, d), F32),
                   jax.ShapeDtypeStruct((n, LANES), jnp.int32),
                   jax.ShapeDtypeStruct((n, LANES), F32)),
        grid=(n // tm,),
        in_specs=[pl.BlockSpec((tm, d), lambda i: (i, 0)),
                  pl.BlockSpec((1, d), lambda i: (0, 0)),
                  pl.BlockSpec((d, LANES), lambda i: (0, 0))],
        out_specs=(pl.BlockSpec((tm, d), lambda i: (i, 0)),
                   pl.BlockSpec((tm, LANES), lambda i: (i, 0)),
                   pl.BlockSpec((tm, LANES), lambda i: (i, 0))),
        compiler_params=_params("parallel"),
        name="rmsnorm_router",
    )(x, g.reshape(1, d), w_router_pad)


def _proj_kernel(a_ref, w_ref, o_ref):
    o_ref[...] = jnp.dot(a_ref[...], w_ref[...], preferred_element_type=F32).astype(o_ref.dtype)


def in_proj(a, w, tm=MM_TILE, tn=MM_TILE):
    m, k = a.shape
    n = w.shape[1]
    return pl.pallas_call(
        _proj_kernel,
        out_shape=jax.ShapeDtypeStruct((m, n), BF16),
        grid=(m // tm, n // tn),
        in_specs=[pl.BlockSpec((tm, k), lambda i, j: (i, 0)),
                  pl.BlockSpec((k, tn), lambda i, j: (0, j))],
        out_specs=pl.BlockSpec((tm, tn), lambda i, j: (i, j)),
        compiler_params=_params("parallel", "parallel"),
        name="in_proj",
    )(a, w)


def _out_proj_kernel(a1_ref, a2_ref, w1_ref, w2_ref, res_ref, o_ref):
    acc = jnp.dot(a1_ref[...], w1_ref[...], preferred_element_type=F32)
    acc += jnp.dot(a2_ref[...], w2_ref[...], preferred_element_type=F32)
    o_ref[...] = res_ref[...] + acc


def out_proj(a1, a2, w, res, tm=MM_TILE, tn=MM_TILE):
    m, k1 = a1.shape
    assert a2.shape[1] == k1 and w.shape[0] == 2 * k1
    n = w.shape[1]
    return pl.pallas_call(
        _out_proj_kernel,
        out_shape=jax.ShapeDtypeStruct((m, n), F32),
        grid=(m // tm, n // tn),
        in_specs=[pl.BlockSpec((tm, k1), lambda i, j: (i, 0)),
                  pl.BlockSpec((tm, k1), lambda i, j: (i, 0)),
                  pl.BlockSpec((k1, tn), lambda i, j: (0, j)),
                  pl.BlockSpec((k1, tn), lambda i, j: (1, j)),
                  pl.BlockSpec((tm, tn), lambda i, j: (i, j))],
        out_specs=pl.BlockSpec((tm, tn), lambda i, j: (i, j)),
        compiler_params=_params("parallel", "parallel"),
        name="out_proj",
    )(a1, a2, w, w, res)


def _silu(x):
    return x * (1.0 / (1.0 + jnp.exp(-x)))


def _ffn_up_kernel(n_ref, wg_ref, wu_ref, o_ref):
    a = n_ref[...]
    g = jnp.dot(a, wg_ref[...], preferred_element_type=F32)
    u = jnp.dot(a, wu_ref[...], preferred_element_type=F32)
    o_ref[...] = (_silu(g) * u).astype(o_ref.dtype)


def ffn_up(n, wg, wu, tm=MM_TILE, tn=512):
    m, k = n.shape
    f = wg.shape[1]
    return pl.pallas_call(
        _ffn_up_kernel,
        out_shape=jax.ShapeDtypeStruct((m, f), BF16),
        grid=(m // tm, f // tn),
        in_specs=[pl.BlockSpec((tm, k), lambda i, j: (i, 0)),
                  pl.BlockSpec((k, tn), lambda i, j: (0, j)),
                  pl.BlockSpec((k, tn), lambda i, j: (0, j))],
        out_specs=pl.BlockSpec((tm, tn), lambda i, j: (i, j)),
        compiler_params=_params("parallel", "parallel"),
        name="ffn_up",
    )(n, wg, wu)


MOE_TILE = 256
GATHER_ROWS = 512


def _route_plan(idx, n_tiles):
    n = idx.shape[0]
    ea = jnp.concatenate([idx[:, 0], idx[:, 1]])
    onehot = (ea[:, None] == jnp.arange(N_EXPERTS, dtype=jnp.int32)[None, :]).astype(jnp.int32)
    incl = jnp.cumsum(onehot, axis=0)
    counts = incl[-1]
    padded = ((counts + MOE_TILE - 1) // MOE_TILE) * MOE_TILE
    ends = jnp.cumsum(padded)
    starts = ends - padded
    pos = jnp.sum(onehot * (starts[None, :] + incl - 1), axis=1)
    token = jnp.arange(2 * n, dtype=jnp.int32) % n
    row_token = jnp.zeros((n_tiles * MOE_TILE,), jnp.int32).at[pos].set(token)
    tile_start = jnp.arange(n_tiles, dtype=jnp.int32) * MOE_TILE
    tile_expert = jnp.minimum(jnp.sum((tile_start[:, None] >= ends[None, :]).astype(jnp.int32), axis=1),
                              N_EXPERTS - 1)
    n_valid = (ends[-1:] // MOE_TILE).astype(jnp.int32)
    return row_token, pos.astype(jnp.int32), tile_expert.astype(jnp.int32), n_valid


def _gather_rows_kernel(idx_ref, src_ref, dst_ref, sem):
    def row_copy(i):
        return pltpu.make_async_copy(src_ref.at[pl.ds(idx_ref[0, 0, i], 1)],
                                     dst_ref.at[pl.ds(i, 1)], sem)

    def start(i, carry):
        row_copy(i).start()
        return carry

    def wait(i, carry):
        row_copy(i).wait()
        return carry

    lax.fori_loop(0, GATHER_ROWS, start, 0, unroll=8)
    lax.fori_loop(0, GATHER_ROWS, wait, 0, unroll=8)


def gather_rows(src, idx):
    r = idx.shape[0]
    steps = r // GATHER_ROWS
    return pl.pallas_call(
        _gather_rows_kernel,
        out_shape=jax.ShapeDtypeStruct((r, src.shape[1]), src.dtype),
        grid=(steps,),
        in_specs=[pl.BlockSpec((1, 1, GATHER_ROWS), lambda i: (i, 0, 0), memory_space=pltpu.SMEM),
                  pl.BlockSpec(memory_space=pl.ANY)],
        out_specs=pl.BlockSpec((GATHER_ROWS, src.shape[1]), lambda i: (i, 0)),
        scratch_shapes=[pltpu.SemaphoreType.DMA],
        compiler_params=_params("arbitrary"),
        name="gather_rows",
    )(idx.reshape(steps, 1, GATHER_ROWS), src)


def _moe_up_kernel(te_ref, nv_ref, x_ref, wg_ref, wu_ref, o_ref):
    valid = pl.program_id(1) < nv_ref[0]

    @pl.when(valid)
    def _():
        a = x_ref[...].astype(BF16)
        g = jnp.dot(a, wg_ref[...], preferred_element_type=F32)
        u = jnp.dot(a, wu_ref[...], preferred_element_type=F32)
        o_ref[...] = (_silu(g) * u).astype(o_ref.dtype)

    @pl.when(jnp.logical_not(valid))
    def _():
        o_ref[...] = jnp.zeros_like(o_ref)


def moe_up(x_rows, tile_expert, n_valid, wg, wu, tn=1024):
    r, k = x_rows.shape
    f = wg.shape[2]
    tm = MOE_TILE
    w_spec = pl.BlockSpec((None, k, tn), lambda j, m, te, nv: (te[m], 0, j))
    return pl.pallas_call(
        _moe_up_kernel,
        out_shape=jax.ShapeDtypeStruct((r, f), BF16),
        grid_spec=pltpu.PrefetchScalarGridSpec(
            num_scalar_prefetch=2,
            grid=(f // tn, r // tm),
            in_specs=[pl.BlockSpec((tm, k), lambda j, m, te, nv: (m, 0)), w_spec, w_spec],
            out_specs=pl.BlockSpec((tm, tn), lambda j, m, te, nv: (m, j))),
        compiler_params=_params("parallel", "arbitrary"),
        name="moe_up",
    )(tile_expert, n_valid, x_rows, wg, wu)


def _moe_down_kernel(te_ref, nv_ref, a_ref, w_ref, o_ref):
    valid = pl.program_id(0) < nv_ref[0]

    @pl.when(valid)
    def _():
        o_ref[...] = jnp.dot(a_ref[...], w_ref[...], preferred_element_type=F32)

    @pl.when(jnp.logical_not(valid))
    def _():
        o_ref[...] = jnp.zeros_like(o_ref)


def moe_down(act_rows, tile_expert, n_valid, wd):
    r, f = act_rows.shape
    d = wd.shape[2]
    tm = MOE_TILE
    return pl.pallas_call(
        _moe_down_kernel,
        out_shape=jax.ShapeDtypeStruct((r, d), F32),
        grid_spec=pltpu.PrefetchScalarGridSpec(
            num_scalar_prefetch=2,
            grid=(r // tm,),
            in_specs=[pl.BlockSpec((tm, f), lambda m, te, nv: (m, 0)),
                      pl.BlockSpec((None, f, d), lambda m, te, nv: (te[m], 0, 0))],
            out_specs=pl.BlockSpec((tm, d), lambda m, te, nv: (m, 0))),
        compiler_params=_params("arbitrary"),
        name="moe_down",
    )(tile_expert, n_valid, act_rows, wd)


def _moe_combine_kernel(h_ref, y1_ref, y2_ref, wts_ref, g_ref, *out_refs, keep_x):
    w = wts_ref[...]
    x = h_ref[...] + w[:, 0:1] * y1_ref[...] + w[:, 1:2] * y2_ref[...]
    if keep_x:
        out_refs[0][...] = x
    r = lax.rsqrt(jnp.mean(x * x, axis=-1, keepdims=True) + EPS)
    out_refs[-1][...] = (x * r * g_ref[...]).astype(out_refs[-1].dtype)


def moe_combine(h, y_pairs, wts, next_norm_g, normed_dtype, keep_x, tm=256):
    n, d = h.shape
    nt = n // tm
    row_spec = pl.BlockSpec((tm, d), lambda i: (i, 0))
    shapes = [jax.ShapeDtypeStruct((n, d), F32)] * keep_x + [jax.ShapeDtypeStruct((n, d), normed_dtype)]
    return pl.pallas_call(
        functools.partial(_moe_combine_kernel, keep_x=keep_x),
        out_shape=tuple(shapes),
        grid=(nt,),
        in_specs=[row_spec, row_spec,
                  pl.BlockSpec((tm, d), lambda i: (i + nt, 0)),
                  pl.BlockSpec((tm, LANES), lambda i: (i, 0)),
                  pl.BlockSpec((1, d), lambda i: (0, 0))],
        out_specs=tuple([row_spec] * len(shapes)),
        compiler_params=_params("parallel"),
        name="moe_combine",
    )(h, y_pairs, y_pairs, wts, next_norm_g.reshape(1, d))


def moe_ffn(h, norm_g, w_router_pad, wg, wu, wd, next_norm_g, normed_dtype, keep_x):
    n = h.shape[0]
    n_tiles = (2 * n) // MOE_TILE + N_EXPERTS
    n2, idx, wts = rmsnorm_router(h, norm_g, w_router_pad)
    row_token, pos, tile_expert, n_valid = _route_plan(idx, n_tiles)
    x_rows = gather_rows(n2, row_token)
    act_rows = moe_up(x_rows, tile_expert, n_valid, wg, wu)
    y_rows = moe_down(act_rows, tile_expert, n_valid, wd)
    y_pairs = gather_rows(y_rows, pos)
    return moe_combine(h, y_pairs, wts, next_norm_g, normed_dtype, keep_x)


def _ffn_down_kernel(a_ref, w_ref, res_ref, o_ref):
    @pl.when(pl.program_id(2) == 0)
    def _():
        o_ref[...] = res_ref[...]

    o_ref[...] += jnp.dot(a_ref[...], w_ref[...], preferred_element_type=F32)


def ffn_down(a, w, res, tm=MM_TILE, tn=MM_TILE, tk=4096):
    m, k = a.shape
    n = w.shape[1]
    return pl.pallas_call(
        _ffn_down_kernel,
        out_shape=jax.ShapeDtypeStruct((m, n), F32),
        grid=(m // tm, n // tn, k // tk),
        in_specs=[pl.BlockSpec((tm, tk), lambda i, j, kk: (i, kk)),
                  pl.BlockSpec((tk, tn), lambda i, j, kk: (kk, j)),
                  pl.BlockSpec((tm, tn), lambda i, j, kk: (i, j))],
        out_specs=pl.BlockSpec((tm, tn), lambda i, j, kk: (i, j)),
        compiler_params=_params("parallel", "parallel", "arbitrary"),
        name="ffn_down",
    )(a, w, res)


def _alibi_slope(h):
    return float(2.0 ** (-8.0 * (h + 1) / SWA_Q_HEADS))


def _swa_bias_table():
    qi = np.arange(WINDOW)[:, None]
    kj = np.arange(2 * WINDOW)[None, :]
    dist = qi - kj + WINDOW
    inside = (dist >= 0) & (dist < WINDOW)
    slopes = np.array([_alibi_slope(h) for h in range(SWA_Q_HEADS)], np.float64)
    bias = -slopes[:, None, None] * dist[None].astype(np.float64)
    later = np.where(inside[None], bias, NEG_BIG)
    first = np.where((inside & (kj >= WINDOW))[None], bias, NEG_BIG)
    table = np.stack([first, later]).astype(np.float32)
    return table.reshape(2, SWA_KV_HEADS, SWA_GROUP * WINDOW, 2 * WINDOW)


def _swa_kernel(q_ref, kp_ref, kc_ref, vp_ref, vc_ref, bias_ref, sink_ref, g_ref, o_ref, acc_ref):
    nt = (((1,), (1,)), ((), ()))
    zeros_half = jnp.zeros((2 * WINDOW, SWA_HEAD_DIM), BF16)
    ones = jnp.ones((2 * WINDOW, LANES), BF16)
    pairs = SWA_GROUP // 2

    for j in range(SWA_KV_HEADS):
        cols = slice(j * SWA_HEAD_DIM, (j + 1) * SWA_HEAD_DIM)
        k_j = jnp.concatenate([kp_ref[:, cols], kc_ref[:, cols]], axis=0)
        v_j = jnp.concatenate([vp_ref[:, cols], vc_ref[:, cols]], axis=0)
        k_pads = (jnp.concatenate([k_j, zeros_half], axis=1), jnp.concatenate([zeros_half, k_j], axis=1))
        v_pads = (jnp.concatenate([v_j, zeros_half], axis=1), jnp.concatenate([zeros_half, v_j], axis=1))
        parts = []
        for p in range(pairs):
            lanes = slice((j * pairs + p) * LANES, (j * pairs + p + 1) * LANES)
            q_pair = q_ref[:, lanes] * jnp.asarray(SWA_HEAD_DIM ** -0.5, BF16)
            for k_pad in k_pads:
                parts.append(lax.dot_general(q_pair, k_pad, nt, preferred_element_type=F32))
        s = jnp.concatenate(parts, axis=0) + bias_ref[j]
        sink = sink_ref[j]
        m = jnp.maximum(jnp.max(s, axis=-1, keepdims=True), sink)
        e = jnp.exp(s - m).astype(BF16)
        denom = jnp.dot(e, ones, preferred_element_type=F32) + jnp.exp(sink - m)
        inv = 1.0 / denom
        for p in range(pairs):
            lanes = slice((j * pairs + p) * LANES, (j * pairs + p + 1) * LANES)
            out_pair = None
            for half in range(2):
                rows = slice((2 * p + half) * WINDOW, (2 * p + half + 1) * WINDOW)
                o = jnp.dot(e[rows], v_pads[half], preferred_element_type=F32) * inv[rows]
                out_pair = o if out_pair is None else out_pair + o
            acc_ref[:, lanes] = out_pair

    acc = acc_ref[...]
    r = lax.rsqrt(jnp.mean(acc * acc, axis=-1, keepdims=True) + EPS)
    o_ref[...] = (acc * r * g_ref[...]).astype(o_ref.dtype)


def swa_attention(proj, sinks, norm_g, batch, seq):
    nb = seq // WINDOW
    rows = lambda b, n: b * nb + n
    prev = lambda b, n: b * nb + jnp.maximum(n - 1, 0)
    kcol = COL_SK // SWA_KV_WIDTH
    vcol = COL_SV // SWA_KV_WIDTH
    gq = SWA_GROUP * WINDOW
    bias = jnp.asarray(_swa_bias_table())
    sink_rows = jnp.repeat(sinks.astype(F32), WINDOW).reshape(SWA_KV_HEADS, gq, 1)
    return pl.pallas_call(
        _swa_kernel,
        out_shape=jax.ShapeDtypeStruct((batch * seq, SWA_WIDTH), BF16),
        grid=(batch, nb),
        in_specs=[pl.BlockSpec((WINDOW, SWA_WIDTH), lambda b, n: (rows(b, n), COL_SQ // SWA_WIDTH)),
                  pl.BlockSpec((WINDOW, SWA_KV_WIDTH), lambda b, n: (prev(b, n), kcol)),
                  pl.BlockSpec((WINDOW, SWA_KV_WIDTH), lambda b, n: (rows(b, n), kcol)),
                  pl.BlockSpec((WINDOW, SWA_KV_WIDTH), lambda b, n: (prev(b, n), vcol)),
                  pl.BlockSpec((WINDOW, SWA_KV_WIDTH), lambda b, n: (rows(b, n), vcol)),
                  pl.BlockSpec((None, SWA_KV_HEADS, gq, 2 * WINDOW),
                               lambda b, n: (jnp.minimum(n, 1), 0, 0, 0)),
                  pl.BlockSpec((SWA_KV_HEADS, gq, 1), lambda b, n: (0, 0, 0)),
                  pl.BlockSpec((1, SWA_WIDTH), lambda b, n: (0, 0))],
        out_specs=pl.BlockSpec((WINDOW, SWA_WIDTH), lambda b, n: (rows(b, n), 0)),
        scratch_shapes=[pltpu.VMEM((WINDOW, SWA_WIDTH), F32)],
        compiler_params=_params("parallel", "parallel"),
        name="swa_attention",
    )(proj, proj, proj, proj, proj, bias, sink_rows, norm_g.reshape(1, SWA_WIDTH))


def _gla_kernel(q_ref, k_ref, v_ref, gg_ref, glr_ref, gw_ref, gb_ref, ng_ref, o_ref, st_ref):
    c = GLA_CHUNK

    @pl.when(pl.program_id(1) == 0)
    def _():
        st_ref[...] = jnp.zeros_like(st_ref)

    logits = jnp.dot(glr_ref[...], gw_ref[...], preferred_element_type=F32) + gb_ref[...]
    log_a = (jnp.minimum(logits, 0.0) - jnp.log(1.0 + jnp.exp(-jnp.abs(logits)))) * (1.0 / GLA_TAU)
    ri = lax.broadcasted_iota(jnp.int32, (c, c), 0)
    ci = lax.broadcasted_iota(jnp.int32, (c, c), 1)
    causal = ri >= ci
    tri = jnp.where(causal, 1.0, 0.0).astype(BF16)
    la_hi = log_a.astype(BF16)
    la_lo = (log_a - la_hi.astype(F32)).astype(BF16)
    b = (jnp.dot(tri, la_hi, preferred_element_type=F32)
         + jnp.dot(tri, la_lo, preferred_element_type=F32))
    b_last = b[c - 1:c, :]
    q = q_ref[...].astype(F32)
    k = k_ref[...].astype(F32)
    q_dec = (q * jnp.exp(b) * (GLA_DK ** -0.5)).astype(BF16)
    k_inv = (k * jnp.exp(-b)).astype(BF16)
    k_end = (k * jnp.exp(b_last - b)).astype(BF16)
    dec = jnp.exp(b_last)

    for h in range(GLA_HEADS):
        kc = slice(h * GLA_DK, (h + 1) * GLA_DK)
        vc = slice(h * GLA_DV, (h + 1) * GLA_DV)
        v = v_ref[:, vc]
        a = lax.dot_general(q_dec[:, kc], k_inv[:, kc], (((1,), (1,)), ((), ())),
                            preferred_element_type=F32)
        a = jnp.where(causal, a, 0.0).astype(BF16)
        st = st_ref[h]
        o = jnp.dot(a, v, preferred_element_type=F32)
        o += lax.dot_general(q_dec[:, kc], st.astype(BF16), (((1,), (1,)), ((), ())),
                             preferred_element_type=F32)
        upd = lax.dot_general(v, k_end[:, kc], (((0,), (0,)), ((), ())),
                              preferred_element_type=F32)
        st_ref[h] = st * dec[:, kc] + upd
        r = lax.rsqrt(jnp.mean(o * o, axis=-1, keepdims=True) + EPS)
        gate = _silu(gg_ref[:, vc].astype(F32))
        o_ref[:, vc] = (o * r * ng_ref[...] * gate).astype(o_ref.dtype)


def gla_attention(proj, gate_w_pad, gate_b, norm_g, batch, seq):
    c = GLA_CHUNK
    nc = seq // c
    rows = lambda b, n: b * nc + n
    return pl.pallas_call(
        _gla_kernel,
        out_shape=jax.ShapeDtypeStruct((batch * seq, GLA_WIDTH), BF16),
        grid=(batch, nc),
        in_specs=[pl.BlockSpec((c, GLA_KEY_WIDTH), lambda b, n: (rows(b, n), COL_GQ // GLA_KEY_WIDTH)),
                  pl.BlockSpec((c, GLA_KEY_WIDTH), lambda b, n: (rows(b, n), COL_GK // GLA_KEY_WIDTH)),
                  pl.BlockSpec((c, GLA_WIDTH), lambda b, n: (rows(b, n), COL_GV // GLA_WIDTH)),
                  pl.BlockSpec((c, GLA_WIDTH), lambda b, n: (rows(b, n), COL_GG // GLA_WIDTH)),
                  pl.BlockSpec((c, LANES), lambda b, n: (rows(b, n), COL_GLR // LANES)),
                  pl.BlockSpec((LANES, GLA_KEY_WIDTH), lambda b, n: (0, 0)),
                  pl.BlockSpec((1, GLA_KEY_WIDTH), lambda b, n: (0, 0)),
                  pl.BlockSpec((1, GLA_DV), lambda b, n: (0, 0))],
        out_specs=pl.BlockSpec((c, GLA_WIDTH), lambda b, n: (rows(b, n), 0)),
        scratch_shapes=[pltpu.VMEM((GLA_HEADS, GLA_DV, GLA_DK), F32)],
        compiler_params=_params("parallel", "arbitrary"),
        name="gla_attention",
    )(proj, proj, proj, proj, proj, gate_w_pad, gate_b.reshape(1, GLA_KEY_WIDTH),
      norm_g.reshape(1, GLA_DV))


CAST_BLOCK_BYTES = 8 * 1024 * 1024


def _cast_kernel(w_ref, o_ref):
    o_ref[...] = w_ref[...].astype(o_ref.dtype)


def cast_weights(w, first, count):
    _, r, c = w.shape
    tr = max(8, min(r, CAST_BLOCK_BYTES // (4 * c)))
    return pl.pallas_call(
        _cast_kernel,
        out_shape=jax.ShapeDtypeStruct((count, r, c), BF16),
        grid=(count, r // tr),
        in_specs=[pl.BlockSpec((None, tr, c), lambda e, i: (first + e, i, 0))],
        out_specs=pl.BlockSpec((None, tr, c), lambda e, i: (e, i, 0)),
        compiler_params=_params("parallel", "parallel"),
        name="cast_weights",
    )(w)


_W_IN_SEGMENTS = ((0, 2048, COL_SQ), (2048, 2304, COL_SK), (2304, 2560, COL_SV), (2560, 3584, COL_GQ),
                  (3584, 4608, COL_GK), (4608, 6656, COL_GV), (6656, 8704, COL_GG))
IN_COLS = 8720


def _prep_w_in_kernel(w_ref, o_ref):
    for lo, hi, dst in _W_IN_SEGMENTS:
        o_ref[:, dst:dst + hi - lo] = w_ref[:, lo:hi].astype(o_ref.dtype)
    rows = w_ref.shape[0]
    tail = w_ref[:, COL_GLR:IN_COLS].astype(o_ref.dtype)
    pad = jnp.zeros((rows, PROJ_COLS - COL_GLR - GLA_GATE_RANK), o_ref.dtype)
    o_ref[:, COL_GLR:PROJ_COLS] = jnp.concatenate([tail, pad], axis=1)


def prep_w_in(w_in, layer, tr=256):
    _, r, c = w_in.shape
    return pl.pallas_call(
        _prep_w_in_kernel,
        out_shape=jax.ShapeDtypeStruct((r, PROJ_COLS), BF16),
        grid=(r // tr,),
        in_specs=[pl.BlockSpec((None, tr, c), lambda i: (layer, i, 0))],
        out_specs=pl.BlockSpec((tr, PROJ_COLS), lambda i: (i, 0)),
        compiler_params=_params("parallel"),
        name="prep_w_in",
    )(w_in)


def kernel(x, attn_norm_g, w_in, gla_gate_w, gla_gate_b, gla_norm_g, swa_sinks, swa_norm_g, w_out,
           ffn_norm_g, dense_w_gate, dense_w_up, dense_w_down, moe_router, moe_w_gate, moe_w_up,
           moe_w_down, final_norm_g):
    batch, seq, d = x.shape
    xs = x.reshape(batch * seq, d)
    moe_shape = (DEPTH // 2 * N_EXPERTS, d, D_FF_EXPERT)
    n = rmsnorm(xs, attn_norm_g[0], BF16)
    for layer in range(DEPTH):
        proj = in_proj(n, prep_w_in(w_in, layer))
        swa = swa_attention(proj, swa_sinks[layer], swa_norm_g[layer], batch, seq)
        gate_w_pad = jnp.pad(gla_gate_w[layer], ((0, LANES - GLA_GATE_RANK), (0, 0))).astype(BF16)
        gla = gla_attention(proj, gate_w_pad, gla_gate_b[layer], gla_norm_g[layer], batch, seq)
        h = out_proj(swa, gla, cast_weights(w_out, layer, 1)[0], xs)
        j = layer // 2
        if layer % 2 == 0:
            n2 = rmsnorm(h, ffn_norm_g[layer], BF16)
            act = ffn_up(n2, cast_weights(dense_w_gate, j, 1)[0], cast_weights(dense_w_up, j, 1)[0])
            xs = ffn_down(act, cast_weights(dense_w_down, j, 1)[0], h)
            n = rmsnorm(xs, attn_norm_g[layer + 1], BF16)
        else:
            wr = jnp.pad(moe_router[j], ((0, 0), (0, LANES - N_EXPERTS))).astype(BF16)
            wg = cast_weights(moe_w_gate.reshape(moe_shape), j * N_EXPERTS, N_EXPERTS)
            wu = cast_weights(moe_w_up.reshape(moe_shape), j * N_EXPERTS, N_EXPERTS)
            wd = cast_weights(moe_w_down.reshape(moe_shape[0], D_FF_EXPERT, d), j * N_EXPERTS, N_EXPERTS)
            if layer + 1 < DEPTH:
                xs, n = moe_ffn(h, ffn_norm_g[layer], wr, wg, wu, wd, attn_norm_g[layer + 1], BF16, True)
            else:
                (out,) = moe_ffn(h, ffn_norm_g[layer], wr, wg, wu, wd, final_norm_g, F32, False)
    return out.reshape(batch, seq, d)
```

```python
import functools

import jax
import jax.numpy as jnp
import numpy as np
from jax import lax
from jax.experimental import pallas as pl
from jax.experimental.pallas import tpu as pltpu

BF16 = jnp.bfloat16
F32 = jnp.float32

D_MODEL = 4096
DEPTH = 4
EPS = 1e-5

SWA_WIDTH = 2048
SWA_HEAD_DIM = 64
SWA_Q_HEADS = 32
SWA_KV_HEADS = 4
SWA_GROUP = 8
SWA_KV_WIDTH = SWA_KV_HEADS * SWA_HEAD_DIM
WINDOW = 128

GLA_WIDTH = 2048
GLA_HEADS = 4
GLA_KEY_WIDTH = 1024
GLA_DK = 256
GLA_DV = 512
GLA_GATE_RANK = 16
GLA_TAU = 16.0
GLA_CHUNK = 64

D_FF_DENSE = 8192
N_EXPERTS = 8
D_FF_EXPERT = 2048

LANES = 128
VMEM_LIMIT = 56 * 1024 * 1024

COL_SQ = 0
COL_GV = 2048
COL_GG = 4096
COL_GQ = 6144
COL_GK = 7168
COL_SK = 8192
COL_SV = 8448
COL_GLR = 8704
PROJ_COLS = 9216

MM_TILE = 1024
NEG_BIG = -1e30


def _params(*sem):
    return pltpu.CompilerParams(dimension_semantics=sem, vmem_limit_bytes=VMEM_LIMIT)


def _rmsnorm_kernel(x_ref, g_ref, o_ref):
    x = x_ref[...]
    r = lax.rsqrt(jnp.mean(x * x, axis=-1, keepdims=True) + EPS)
    o_ref[...] = (x * r * g_ref[...]).astype(o_ref.dtype)


def rmsnorm(x, g, out_dtype, tm=512):
    n, d = x.shape
    return pl.pallas_call(
        _rmsnorm_kernel,
        out_shape=jax.ShapeDtypeStruct((n, d), out_dtype),
        grid=(n // tm,),
        in_specs=[pl.BlockSpec((tm, d), lambda i: (i, 0)),
                  pl.BlockSpec((1, d), lambda i: (0, 0))],
        out_specs=pl.BlockSpec((tm, d), lambda i: (i, 0)),
        compiler_params=_params("parallel"),
        name="rmsnorm",
    )(x, g.reshape(1, d))


def _rmsnorm_router_kernel(x_ref, g_ref, wr_ref, o_ref, idx_ref, wts_ref):
    x = x_ref[...]
    r = lax.rsqrt(jnp.mean(x * x, axis=-1, keepdims=True) + EPS)
    n = x * r * g_ref[...]
    o_ref[...] = n
    logits = jnp.dot(n.astype(BF16), wr_ref[...], preferred_element_type=F32)
    lane = lax.broadcasted_iota(jnp.int32, logits.shape, 1)
    logits = jnp.where(lane < N_EXPERTS, logits, NEG_BIG)
    v1 = jnp.max(logits, axis=-1, keepdims=True)
    i1 = jnp.min(jnp.where(logits == v1, lane, LANES), axis=-1, keepdims=True)
    rest = jnp.where(lane == i1, NEG_BIG, logits)
    v2 = jnp.max(rest, axis=-1, keepdims=True)
    i2 = jnp.min(jnp.where(rest == v2, lane, LANES), axis=-1, keepdims=True)
    e2 = jnp.exp(v2 - v1)
    w1 = 1.0 / (1.0 + e2)
    w2 = e2 / (1.0 + e2)
    idx_ref[...] = jnp.where(lane == 0, i1, i2)
    wts_ref[...] = jnp.where(lane == 0, w1, w2)


def rmsnorm_router(x, g, w_router_pad, tm=512):
    n, d = x.shape
    return pl.pallas_call(
        _rmsnorm_router_kernel,
        out_shape=(jax.ShapeDtypeStruct((n, d), F32),
                   jax.ShapeDtypeStruct((n, LANES), jnp.int32),
                   jax.ShapeDtypeStruct((n, LANES), F32)),
        grid=(n // tm,),
        in_specs=[pl.BlockSpec((tm, d), lambda i: (i, 0)),
                  pl.BlockSpec((1, d), lambda i: (0, 0)),
                  pl.BlockSpec((d, LANES), lambda i: (0, 0))],
        out_specs=(pl.BlockSpec((tm, d), lambda i: (i, 0)),
                   pl.BlockSpec((tm, LANES), lambda i: (i, 0)),
                   pl.BlockSpec((tm, LANES), lambda i: (i, 0))),
        compiler_params=_params("parallel"),
        name="rmsnorm_router",
    )(x, g.reshape(1, d), w_router_pad)


def _proj_kernel(a_ref, w_ref, o_ref):
    o_ref[...] = jnp.dot(a_ref[...], w_ref[...], preferred_element_type=F32).astype(o_ref.dtype)


def in_proj(a, w, tm=MM_TILE, tn=MM_TILE):
    m, k = a.shape
    n = w.shape[1]
    return pl.pallas_call(
        _proj_kernel,
        out_shape=jax.ShapeDtypeStruct((m, n), BF16),
        grid=(m // tm, n // tn),
        in_specs=[pl.BlockSpec((tm, k), lambda i, j: (i, 0)),
                  pl.BlockSpec((k, tn), lambda i, j: (0, j))],
        out_specs=pl.BlockSpec((tm, tn), lambda i, j: (i, j)),
        compiler_params=_params("parallel", "parallel"),
        name="in_proj",
    )(a, w)


def _out_proj_kernel(a1_ref, a2_ref, w1_ref, w2_ref, res_ref, o_ref):
    acc = jnp.dot(a1_ref[...], w1_ref[...], preferred_element_type=F32)
    acc += jnp.dot(a2_ref[...], w2_ref[...], preferred_element_type=F32)
    o_ref[...] = res_ref[...] + acc


def out_proj(a1, a2, w, res, tm=MM_TILE, tn=MM_TILE):
    m, k1 = a1.shape
    assert a2.shape[1] == k1 and w.shape[0] == 2 * k1
    n = w.shape[1]
    return pl.pallas_call(
        _out_proj_kernel,
        out_shape=jax.ShapeDtypeStruct((m, n), F32),
        grid=(m // tm, n // tn),
        in_specs=[pl.BlockSpec((tm, k1), lambda i, j: (i, 0)),
                  pl.BlockSpec((tm, k1), lambda i, j: (i, 0)),
                  pl.BlockSpec((k1, tn), lambda i, j: (0, j)),
                  pl.BlockSpec((k1, tn), lambda i, j: (1, j)),
                  pl.BlockSpec((tm, tn), lambda i, j: (i, j))],
        out_specs=pl.BlockSpec((tm, tn), lambda i, j: (i, j)),
        compiler_params=_params("parallel", "parallel"),
        name="out_proj",
    )(a1, a2, w, w, res)


def _silu(x):
    return x * (1.0 / (1.0 + jnp.exp(-x)))


def _ffn_up_kernel(n_ref, wg_ref, wu_ref, o_ref):
    a = n_ref[...]
    g = jnp.dot(a, wg_ref[...], preferred_element_type=F32)
    u = jnp.dot(a, wu_ref[...], preferred_element_type=F32)
    o_ref[...] = (_silu(g) * u).astype(o_ref.dtype)


def ffn_up(n, wg, wu, tm=MM_TILE, tn=512):
    m, k = n.shape
    f = wg.shape[1]
    return pl.pallas_call(
        _ffn_up_kernel,
        out_shape=jax.ShapeDtypeStruct((m, f), BF16),
        grid=(m // tm, f // tn),
        in_specs=[pl.BlockSpec((tm, k), lambda i, j: (i, 0)),
                  pl.BlockSpec((k, tn), lambda i, j: (0, j)),
                  pl.BlockSpec((k, tn), lambda i, j: (0, j))],
        out_specs=pl.BlockSpec((tm, tn), lambda i, j: (i, j)),
        compiler_params=_params("parallel", "parallel"),
        name="ffn_up",
    )(n, wg, wu)


MOE_TILE = 256
GATHER_ROWS = 512


def _route_plan(idx, n_tiles):
    n = idx.shape[0]
    ea = jnp.concatenate([idx[:, 0], idx[:, 1]])
    onehot = (ea[:, None] == jnp.arange(N_EXPERTS, dtype=jnp.int32)[None, :]).astype(jnp.int32)
    incl = jnp.cumsum(onehot, axis=0)
    counts = incl[-1]
    padded = ((counts + MOE_TILE - 1) // MOE_TILE) * MOE_TILE
    ends = jnp.cumsum(padded)
    starts = ends - padded
    pos = jnp.sum(onehot * (starts[None, :] + incl - 1), axis=1)
    token = jnp.arange(2 * n, dtype=jnp.int32) % n
    row_token = jnp.zeros((n_tiles * MOE_TILE,), jnp.int32).at[pos].set(token)
    tile_start = jnp.arange(n_tiles, dtype=jnp.int32) * MOE_TILE
    tile_expert = jnp.minimum(jnp.sum((tile_start[:, None] >= ends[None, :]).astype(jnp.int32), axis=1),
                              N_EXPERTS - 1)
    n_valid = (ends[-1:] // MOE_TILE).astype(jnp.int32)
    return row_token, pos.astype(jnp.int32), tile_expert.astype(jnp.int32), n_valid


def _row_dmas(idx_ref, src_ref, buf_ref, sem, count, buf_base, wait):
    def body(i, carry):
        copy = pltpu.make_async_copy(src_ref.at[pl.ds(idx_ref[0, 0, i], 1)],
                                     buf_ref.at[pl.ds(buf_base + i, 1)], sem)
        if wait:
            copy.wait()
        else:
            copy.start()
        return carry

    lax.fori_loop(0, count, body, 0, unroll=8)


def _gather_cast_kernel(idx_ref, src_ref, dst_ref, buf_ref, sem):
    _row_dmas(idx_ref, src_ref, buf_ref, sem, GATHER_ROWS, 0, wait=False)
    _row_dmas(idx_ref, src_ref, buf_ref, sem, GATHER_ROWS, 0, wait=True)
    dst_ref[...] = buf_ref[...].astype(dst_ref.dtype)


def gather_rows_bf16(src, idx):
    r = idx.shape[0]
    d = src.shape[1]
    steps = r // GATHER_ROWS
    return pl.pallas_call(
        _gather_cast_kernel,
        out_shape=jax.ShapeDtypeStruct((r, d), BF16),
        grid=(steps,),
        in_specs=[pl.BlockSpec((1, 1, GATHER_ROWS), lambda i: (i, 0, 0), memory_space=pltpu.SMEM),
                  pl.BlockSpec(memory_space=pl.ANY)],
        out_specs=pl.BlockSpec((GATHER_ROWS, d), lambda i: (i, 0)),
        scratch_shapes=[pltpu.VMEM((GATHER_ROWS, d), src.dtype), pltpu.SemaphoreType.DMA],
        compiler_params=_params("arbitrary"),
        name="gather_rows",
    )(idx.reshape(steps, 1, GATHER_ROWS), src)


def _moe_up_kernel(te_ref, nv_ref, x_ref, wg_ref, wu_ref, o_ref):
    valid = pl.program_id(1) < nv_ref[0]

    @pl.when(valid)
    def _():
        a = x_ref[...]
        g = jnp.dot(a, wg_ref[...], preferred_element_type=F32)
        u = jnp.dot(a, wu_ref[...], preferred_element_type=F32)
        o_ref[...] = (_silu(g) * u).astype(o_ref.dtype)

    @pl.when(jnp.logical_not(valid))
    def _():
        o_ref[...] = jnp.zeros_like(o_ref)


def moe_up(x_rows, tile_expert, n_valid, wg, wu, tn=1024):
    r, k = x_rows.shape
    f = wg.shape[2]
    tm = MOE_TILE
    w_spec = pl.BlockSpec((None, k, tn), lambda j, m, te, nv: (te[m], 0, j))
    return pl.pallas_call(
        _moe_up_kernel,
        out_shape=jax.ShapeDtypeStruct((r, f), BF16),
        grid_spec=pltpu.PrefetchScalarGridSpec(
            num_scalar_prefetch=2,
            grid=(f // tn, r // tm),
            in_specs=[pl.BlockSpec((tm, k), lambda j, m, te, nv: (m, 0)), w_spec, w_spec],
            out_specs=pl.BlockSpec((tm, tn), lambda j, m, te, nv: (m, j))),
        compiler_params=_params("parallel", "arbitrary"),
        name="moe_up",
    )(tile_expert, n_valid, x_rows, wg, wu)


def _moe_down_kernel(te_ref, nv_ref, a_ref, w_ref, o_ref):
    valid = pl.program_id(0) < nv_ref[0]

    @pl.when(valid)
    def _():
        o_ref[...] = jnp.dot(a_ref[...], w_ref[...], preferred_element_type=F32)

    @pl.when(jnp.logical_not(valid))
    def _():
        o_ref[...] = jnp.zeros_like(o_ref)


def moe_down(act_rows, tile_expert, n_valid, wd):
    r, f = act_rows.shape
    d = wd.shape[2]
    tm = MOE_TILE
    return pl.pallas_call(
        _moe_down_kernel,
        out_shape=jax.ShapeDtypeStruct((r, d), F32),
        grid_spec=pltpu.PrefetchScalarGridSpec(
            num_scalar_prefetch=2,
            grid=(r // tm,),
            in_specs=[pl.BlockSpec((tm, f), lambda m, te, nv: (m, 0)),
                      pl.BlockSpec((None, f, d), lambda m, te, nv: (te[m], 0, 0))],
            out_specs=pl.BlockSpec((tm, d), lambda m, te, nv: (m, 0))),
        compiler_params=_params("arbitrary"),
        name="moe_down",
    )(tile_expert, n_valid, act_rows, wd)


def _moe_combine_kernel(pos1_ref, pos2_ref, h_ref, y_ref, wts_ref, g_ref, *refs, keep_x):
    out_refs, (ybuf_ref, sem) = refs[:-2], refs[-2:]
    tm = h_ref.shape[0]
    for wait in (False, True):
        _row_dmas(pos1_ref, y_ref, ybuf_ref, sem, tm, 0, wait)
        _row_dmas(pos2_ref, y_ref, ybuf_ref, sem, tm, tm, wait)
    w = wts_ref[...]
    x = h_ref[...] + w[:, 0:1] * ybuf_ref[0:tm, :] + w[:, 1:2] * ybuf_ref[tm:2 * tm, :]
    if keep_x:
        out_refs[0][...] = x
    r = lax.rsqrt(jnp.mean(x * x, axis=-1, keepdims=True) + EPS)
    out_refs[-1][...] = (x * r * g_ref[...]).astype(out_refs[-1].dtype)


def moe_combine(h, y_rows, pos, wts, next_norm_g, normed_dtype, keep_x, tm=256):
    n, d = h.shape
    nt = n // tm
    row_spec = pl.BlockSpec((tm, d), lambda i: (i, 0))
    shapes = [jax.ShapeDtypeStruct((n, d), F32)] * keep_x + [jax.ShapeDtypeStruct((n, d), normed_dtype)]
    pos_blocks = pos.reshape(2 * nt, 1, tm)
    return pl.pallas_call(
        functools.partial(_moe_combine_kernel, keep_x=keep_x),
        out_shape=tuple(shapes),
        grid=(nt,),
        in_specs=[pl.BlockSpec((1, 1, tm), lambda i: (i, 0, 0), memory_space=pltpu.SMEM),
                  pl.BlockSpec((1, 1, tm), lambda i: (i + nt, 0, 0), memory_space=pltpu.SMEM),
                  row_spec,
                  pl.BlockSpec(memory_space=pl.ANY),
                  pl.BlockSpec((tm, LANES), lambda i: (i, 0)),
                  pl.BlockSpec((1, d), lambda i: (0, 0))],
        out_specs=tuple([row_spec] * len(shapes)),
        scratch_shapes=[pltpu.VMEM((2 * tm, d), y_rows.dtype), pltpu.SemaphoreType.DMA],
        compiler_params=_params("arbitrary"),
        name="moe_combine",
    )(pos_blocks, pos_blocks, h, y_rows, wts, next_norm_g.reshape(1, d))


def moe_ffn(h, norm_g, w_router_pad, wg, wu, wd, next_norm_g, normed_dtype, keep_x):
    n = h.shape[0]
    n_tiles = (2 * n) // MOE_TILE + N_EXPERTS
    n2, idx, wts = rmsnorm_router(h, norm_g, w_router_pad)
    row_token, pos, tile_expert, n_valid = _route_plan(idx, n_tiles)
    x_rows = gather_rows_bf16(n2, row_token)
    act_rows = moe_up(x_rows, tile_expert, n_valid, wg, wu)
    y_rows = moe_down(act_rows, tile_expert, n_valid, wd)
    return moe_combine(h, y_rows, pos, wts, next_norm_g, normed_dtype, keep_x)


def _ffn_down_kernel(a_ref, w_ref, res_ref, o_ref):
    @pl.when(pl.program_id(2) == 0)
    def _():
        o_ref[...] = res_ref[...]

    o_ref[...] += jnp.dot(a_ref[...], w_ref[...], preferred_element_type=F32)


def ffn_down(a, w, res, tm=MM_TILE, tn=MM_TILE, tk=4096):
    m, k = a.shape
    n = w.shape[1]
    return pl.pallas_call(
        _ffn_down_kernel,
        out_shape=jax.ShapeDtypeStruct((m, n), F32),
        grid=(m // tm, n // tn, k // tk),
        in_specs=[pl.BlockSpec((tm, tk), lambda i, j, kk: (i, kk)),
                  pl.BlockSpec((tk, tn), lambda i, j, kk: (kk, j)),
                  pl.BlockSpec((tm, tn), lambda i, j, kk: (i, j))],
        out_specs=pl.BlockSpec((tm, tn), lambda i, j, kk: (i, j)),
        compiler_params=_params("parallel", "parallel", "arbitrary"),
        name="ffn_down",
    )(a, w, res)


def _alibi_slope(h):
    return float(2.0 ** (-8.0 * (h + 1) / SWA_Q_HEADS))


def _swa_bias_table():
    qi = np.arange(WINDOW)[:, None]
    kj = np.arange(2 * WINDOW)[None, :]
    dist = qi - kj + WINDOW
    inside = (dist >= 0) & (dist < WINDOW)
    slopes = np.array([_alibi_slope(h) for h in range(SWA_Q_HEADS)], np.float64)
    bias = -slopes[:, None, None] * dist[None].astype(np.float64)
    later = np.where(inside[None], bias, NEG_BIG)
    first = np.where((inside & (kj >= WINDOW))[None], bias, NEG_BIG)
    table = np.stack([first, later]).astype(np.float32)
    return table.reshape(2, SWA_KV_HEADS, SWA_GROUP * WINDOW, 2 * WINDOW)


def _swa_kernel(q_ref, kp_ref, kc_ref, vp_ref, vc_ref, bias_ref, sink_ref, g_ref, o_ref, acc_ref):
    nt = (((1,), (1,)), ((), ()))
    zeros_half = jnp.zeros((2 * WINDOW, SWA_HEAD_DIM), BF16)
    ones = jnp.ones((2 * WINDOW, LANES), BF16)
    pairs = SWA_GROUP // 2

    def padded(ref_prev, ref_cur, j):
        cols = slice(j * SWA_HEAD_DIM, (j + 1) * SWA_HEAD_DIM)
        x = jnp.concatenate([ref_prev[:, cols], ref_cur[:, cols]], axis=0)
        return jnp.concatenate([x, zeros_half], axis=1), jnp.concatenate([zeros_half, x], axis=1)

    scores, maxes = [], []
    for j in range(SWA_KV_HEADS):
        k_pads = padded(kp_ref, kc_ref, j)
        parts = []
        for p in range(pairs):
            lanes = slice((j * pairs + p) * LANES, (j * pairs + p + 1) * LANES)
            q_pair = q_ref[:, lanes] * jnp.asarray(SWA_HEAD_DIM ** -0.5, BF16)
            for k_pad in k_pads:
                parts.append(lax.dot_general(q_pair, k_pad, nt, preferred_element_type=F32))
        s = jnp.concatenate(parts, axis=0) + bias_ref[j]
        scores.append(s)
        maxes.append(jnp.maximum(jnp.max(s, axis=-1, keepdims=True), sink_ref[j]))

    probs, invs = [], []
    for j in range(SWA_KV_HEADS):
        m = maxes[j]
        e = jnp.exp(scores[j] - jnp.concatenate([m, m], axis=1)).astype(BF16)
        denom = jnp.dot(e, ones, preferred_element_type=F32) + jnp.exp(sink_ref[j] - m)
        probs.append(e)
        invs.append(1.0 / denom)

    for j in range(SWA_KV_HEADS):
        v_pads = padded(vp_ref, vc_ref, j)
        for p in range(pairs):
            lanes = slice((j * pairs + p) * LANES, (j * pairs + p + 1) * LANES)
            out_pair = None
            for half in range(2):
                rows = slice((2 * p + half) * WINDOW, (2 * p + half + 1) * WINDOW)
                o = jnp.dot(probs[j][rows], v_pads[half], preferred_element_type=F32) * invs[j][rows]
                out_pair = o if out_pair is None else out_pair + o
            acc_ref[:, lanes] = out_pair

    acc = acc_ref[...]
    r = lax.rsqrt(jnp.mean(acc * acc, axis=-1, keepdims=True) + EPS)
    o_ref[...] = (acc * r * g_ref[...]).astype(o_ref.dtype)


def swa_attention(proj, sinks, norm_g, batch, seq):
    nb = seq // WINDOW
    rows = lambda b, n: b * nb + n
    prev = lambda b, n: b * nb + jnp.maximum(n - 1, 0)
    kcol = COL_SK // SWA_KV_WIDTH
    vcol = COL_SV // SWA_KV_WIDTH
    gq = SWA_GROUP * WINDOW
    bias = jnp.asarray(_swa_bias_table())
    sink_rows = jnp.broadcast_to(jnp.repeat(sinks.astype(F32), WINDOW).reshape(SWA_KV_HEADS, gq, 1),
                                 (SWA_KV_HEADS, gq, LANES))
    return pl.pallas_call(
        _swa_kernel,
        out_shape=jax.ShapeDtypeStruct((batch * seq, SWA_WIDTH), BF16),
        grid=(batch, nb),
        in_specs=[pl.BlockSpec((WINDOW, SWA_WIDTH), lambda b, n: (rows(b, n), COL_SQ // SWA_WIDTH)),
                  pl.BlockSpec((WINDOW, SWA_KV_WIDTH), lambda b, n: (prev(b, n), kcol)),
                  pl.BlockSpec((WINDOW, SWA_KV_WIDTH), lambda b, n: (rows(b, n), kcol)),
                  pl.BlockSpec((WINDOW, SWA_KV_WIDTH), lambda b, n: (prev(b, n), vcol)),
                  pl.BlockSpec((WINDOW, SWA_KV_WIDTH), lambda b, n: (rows(b, n), vcol)),
                  pl.BlockSpec((None, SWA_KV_HEADS, gq, 2 * WINDOW),
                               lambda b, n: (jnp.minimum(n, 1), 0, 0, 0)),
                  pl.BlockSpec((SWA_KV_HEADS, gq, LANES), lambda b, n: (0, 0, 0)),
                  pl.BlockSpec((1, SWA_WIDTH), lambda b, n: (0, 0))],
        out_specs=pl.BlockSpec((WINDOW, SWA_WIDTH), lambda b, n: (rows(b, n), 0)),
        scratch_shapes=[pltpu.VMEM((WINDOW, SWA_WIDTH), F32)],
        compiler_params=_params("parallel", "parallel"),
        name="swa_attention",
    )(proj, proj, proj, proj, proj, bias, sink_rows, norm_g.reshape(1, SWA_WIDTH))


def _gla_kernel(q_ref, k_ref, v_ref, gg_ref, glr_ref, gw_ref, gb_ref, ng_ref, o_ref, st_ref):
    c = GLA_CHUNK

    @pl.when(pl.program_id(1) == 0)
    def _():
        st_ref[...] = jnp.zeros_like(st_ref)

    logits = jnp.dot(glr_ref[...], gw_ref[...], preferred_element_type=F32) + gb_ref[...]
    log_a = (jnp.minimum(logits, 0.0) - jnp.log(1.0 + jnp.exp(-jnp.abs(logits)))) * (1.0 / GLA_TAU)
    ri = lax.broadcasted_iota(jnp.int32, (c, c), 0)
    ci = lax.broadcasted_iota(jnp.int32, (c, c), 1)
    causal = ri >= ci
    tri = jnp.where(causal, 1.0, 0.0).astype(BF16)
    la_hi = log_a.astype(BF16)
    la_lo = (log_a - la_hi.astype(F32)).astype(BF16)
    b = (jnp.dot(tri, la_hi, preferred_element_type=F32)
         + jnp.dot(tri, la_lo, preferred_element_type=F32))
    b_last = b[c - 1:c, :]
    q = q_ref[...].astype(F32)
    k = k_ref[...].astype(F32)
    q_dec = (q * jnp.exp(b) * (GLA_DK ** -0.5)).astype(BF16)
    k_inv = (k * jnp.exp(-b)).astype(BF16)
    k_end = (k * jnp.exp(b_last - b)).astype(BF16)
    dec = jnp.exp(b_last)

    for h in range(GLA_HEADS):
        kc = slice(h * GLA_DK, (h + 1) * GLA_DK)
        vc = slice(h * GLA_DV, (h + 1) * GLA_DV)
        v = v_ref[:, vc]
        a = lax.dot_general(q_dec[:, kc], k_inv[:, kc], (((1,), (1,)), ((), ())),
                            preferred_element_type=F32)
        a = jnp.where(causal, a, 0.0).astype(BF16)
        st = st_ref[h]
        o = jnp.dot(a, v, preferred_element_type=F32)
        o += lax.dot_general(q_dec[:, kc], st.astype(BF16), (((1,), (1,)), ((), ())),
                             preferred_element_type=F32)
        upd = lax.dot_general(v, k_end[:, kc], (((0,), (0,)), ((), ())),
                              preferred_element_type=F32)
        st_ref[h] = st * dec[:, kc] + upd
        r = lax.rsqrt(jnp.mean(o * o, axis=-1, keepdims=True) + EPS)
        gate = _silu(gg_ref[:, vc].astype(F32))
        o_ref[:, vc] = (o * r * ng_ref[...] * gate).astype(o_ref.dtype)


def gla_attention(proj, gate_w_pad, gate_b, norm_g, batch, seq):
    c = GLA_CHUNK
    nc = seq // c
    rows = lambda b, n: b * nc + n
    return pl.pallas_call(
        _gla_kernel,
        out_shape=jax.ShapeDtypeStruct((batch * seq, GLA_WIDTH), BF16),
        grid=(batch, nc),
        in_specs=[pl.BlockSpec((c, GLA_KEY_WIDTH), lambda b, n: (rows(b, n), COL_GQ // GLA_KEY_WIDTH)),
                  pl.BlockSpec((c, GLA_KEY_WIDTH), lambda b, n: (rows(b, n), COL_GK // GLA_KEY_WIDTH)),
                  pl.BlockSpec((c, GLA_WIDTH), lambda b, n: (rows(b, n), COL_GV // GLA_WIDTH)),
                  pl.BlockSpec((c, GLA_WIDTH), lambda b, n: (rows(b, n), COL_GG // GLA_WIDTH)),
                  pl.BlockSpec((c, LANES), lambda b, n: (rows(b, n), COL_GLR // LANES)),
                  pl.BlockSpec((LANES, GLA_KEY_WIDTH), lambda b, n: (0, 0)),
                  pl.BlockSpec((1, GLA_KEY_WIDTH), lambda b, n: (0, 0)),
                  pl.BlockSpec((1, GLA_DV), lambda b, n: (0, 0))],
        out_specs=pl.BlockSpec((c, GLA_WIDTH), lambda b, n: (rows(b, n), 0)),
        scratch_shapes=[pltpu.VMEM((GLA_HEADS, GLA_DV, GLA_DK), F32)],
        compiler_params=_params("parallel", "arbitrary"),
        name="gla_attention",
    )(proj, proj, proj, proj, proj, gate_w_pad, gate_b.reshape(1, GLA_KEY_WIDTH),
      norm_g.reshape(1, GLA_DV))


CAST_BLOCK_BYTES = 8 * 1024 * 1024


def _cast_kernel(w_ref, o_ref):
    o_ref[...] = w_ref[...].astype(o_ref.dtype)


def cast_weights(w, first, count):
    _, r, c = w.shape
    tr = max(8, min(r, CAST_BLOCK_BYTES // (4 * c)))
    return pl.pallas_call(
        _cast_kernel,
        out_shape=jax.ShapeDtypeStruct((count, r, c), BF16),
        grid=(count, r // tr),
        in_specs=[pl.BlockSpec((None, tr, c), lambda e, i: (first + e, i, 0))],
        out_specs=pl.BlockSpec((None, tr, c), lambda e, i: (e, i, 0)),
        compiler_params=_params("parallel", "parallel"),
        name="cast_weights",
    )(w)


_W_IN_SEGMENTS = ((0, 2048, COL_SQ), (2048, 2304, COL_SK), (2304, 2560, COL_SV), (2560, 3584, COL_GQ),
                  (3584, 4608, COL_GK), (4608, 6656, COL_GV), (6656, 8704, COL_GG))
IN_COLS = 8720
PREP_COLS = 256


def _w_in_source_blocks():
    table = np.zeros((PROJ_COLS // PREP_COLS,), np.int32)
    for lo, hi, dst in _W_IN_SEGMENTS:
        for k in range((hi - lo) // PREP_COLS):
            table[dst // PREP_COLS + k] = lo // PREP_COLS + k
    return table


def _prep_w_in_kernel(tbl_ref, wt_ref, tail_ref, o_ref):
    jb = pl.program_id(0)
    full_blocks = COL_GLR // PREP_COLS

    @pl.when(jb < full_blocks)
    def _():
        o_ref[...] = wt_ref[...].T.astype(o_ref.dtype)

    @pl.when(jb == full_blocks)
    def _():
        o_ref[:, 0:LANES] = tail_ref[...].T.astype(o_ref.dtype)
        o_ref[:, LANES:PREP_COLS] = jnp.zeros((o_ref.shape[0], PREP_COLS - LANES), o_ref.dtype)

    @pl.when(jb > full_blocks)
    def _():
        o_ref[...] = jnp.zeros_like(o_ref)


def prep_w_in(w_in, layer):
    _, d, _ = w_in.shape
    w_t = jnp.transpose(w_in, (0, 2, 1))
    tail = jnp.pad(w_t[layer, COL_GLR:IN_COLS], ((0, LANES - GLA_GATE_RANK), (0, 0)))
    return pl.pallas_call(
        _prep_w_in_kernel,
        out_shape=jax.ShapeDtypeStruct((d, PROJ_COLS), BF16),
        grid_spec=pltpu.PrefetchScalarGridSpec(
            num_scalar_prefetch=1,
            grid=(PROJ_COLS // PREP_COLS,),
            in_specs=[pl.BlockSpec((None, PREP_COLS, d), lambda j, tbl: (layer, tbl[j], 0)),
                      pl.BlockSpec((LANES, d), lambda j, tbl: (0, 0))],
            out_specs=pl.BlockSpec((d, PREP_COLS), lambda j, tbl: (0, j))),
        compiler_params=_params("arbitrary"),
        name="prep_w_in",
    )(jnp.asarray(_w_in_source_blocks()), w_t, tail)


def kernel(x, attn_norm_g, w_in, gla_gate_w, gla_gate_b, gla_norm_g, swa_sinks, swa_norm_g, w_out,
           ffn_norm_g, dense_w_gate, dense_w_up, dense_w_down, moe_router, moe_w_gate, moe_w_up,
           moe_w_down, final_norm_g):
    batch, seq, d = x.shape
    xs = x.reshape(batch * seq, d)
    moe_shape = (DEPTH // 2 * N_EXPERTS, d, D_FF_EXPERT)
    n = rmsnorm(xs, attn_norm_g[0], BF16)
    for layer in range(DEPTH):
        proj = in_proj(n, prep_w_in(w_in, layer))
        swa = swa_attention(proj, swa_sinks[layer], swa_norm_g[layer], batch, seq)
        gate_w_pad = jnp.pad(gla_gate_w[layer], ((0, LANES - GLA_GATE_RANK), (0, 0))).astype(BF16)
        gla = gla_attention(proj, gate_w_pad, gla_gate_b[layer], gla_norm_g[layer], batch, seq)
        h = out_proj(swa, gla, cast_weights(w_out, layer, 1)[0], xs)
        j = layer // 2
        if layer % 2 == 0:
            n2 = rmsnorm(h, ffn_norm_g[layer], BF16)
            act = ffn_up(n2, cast_weights(dense_w_gate, j, 1)[0], cast_weights(dense_w_up, j, 1)[0])
            xs = ffn_down(act, cast_weights(dense_w_down, j, 1)[0], h)
            n = rmsnorm(xs, attn_norm_g[layer + 1], BF16)
        else:
            wr = jnp.pad(moe_router[j], ((0, 0), (0, LANES - N_EXPERTS))).astype(BF16)
            wg = cast_weights(moe_w_gate.reshape(moe_shape), j * N_EXPERTS, N_EXPERTS)
            wu = cast_weights(moe_w_up.reshape(moe_shape), j * N_EXPERTS, N_EXPERTS)
            wd = cast_weights(moe_w_down.reshape(moe_shape[0], D_FF_EXPERT, d), j * N_EXPERTS, N_EXPERTS)
            if layer + 1 < DEPTH:
                xs, n = moe_ffn(h, ffn_norm_g[layer], wr, wg, wu, wd, attn_norm_g[layer + 1], BF16, True)
            else:
                (out,) = moe_ffn(h, ffn_norm_g[layer], wr, wg, wu, wd, final_norm_g, F32, False)
    return out.reshape(batch, seq, d)
```

```python
import functools

import jax
import jax.numpy as jnp
import numpy as np
from jax import lax
from jax.experimental import pallas as pl
from jax.experimental.pallas import tpu as pltpu

BF16 = jnp.bfloat16
F32 = jnp.float32

D_MODEL = 4096
DEPTH = 4
EPS = 1e-5

SWA_WIDTH = 2048
SWA_HEAD_DIM = 64
SWA_Q_HEADS = 32
SWA_KV_HEADS = 4
SWA_GROUP = 8
SWA_KV_WIDTH = SWA_KV_HEADS * SWA_HEAD_DIM
WINDOW = 128

GLA_WIDTH = 2048
GLA_HEADS = 4
GLA_KEY_WIDTH = 1024
GLA_DK = 256
GLA_DV = 512
GLA_GATE_RANK = 16
GLA_TAU = 16.0
GLA_CHUNK = 64

D_FF_DENSE = 8192
N_EXPERTS = 8
D_FF_EXPERT = 2048

LANES = 128
VMEM_LIMIT = 56 * 1024 * 1024

COL_SQ = 0
COL_GV = 2048
COL_GG = 4096
COL_GQ = 6144
COL_GK = 7168
COL_SK = 8192
COL_SV = 8448
COL_GLR = 8704
PROJ_COLS = 9216

MM_TILE = 1024
NEG_BIG = -1e30


def _params(*sem):
    return pltpu.CompilerParams(dimension_semantics=sem, vmem_limit_bytes=VMEM_LIMIT)


def _rmsnorm_kernel(x_ref, g_ref, o_ref):
    x = x_ref[...]
    r = lax.rsqrt(jnp.mean(x * x, axis=-1, keepdims=True) + EPS)
    o_ref[...] = (x * r * g_ref[...]).astype(o_ref.dtype)


def rmsnorm(x, g, out_dtype, tm=512):
    n, d = x.shape
    return pl.pallas_call(
        _rmsnorm_kernel,
        out_shape=jax.ShapeDtypeStruct((n, d), out_dtype),
        grid=(n // tm,),
        in_specs=[pl.BlockSpec((tm, d), lambda i: (i, 0)),
                  pl.BlockSpec((1, d), lambda i: (0, 0))],
        out_specs=pl.BlockSpec((tm, d), lambda i: (i, 0)),
        compiler_params=_params("parallel"),
        name="rmsnorm",
    )(x, g.reshape(1, d))


def _rmsnorm_router_kernel(x_ref, g_ref, wr_ref, o_ref, idx_ref, wts_ref):
    x = x_ref[...]
    r = lax.rsqrt(jnp.mean(x * x, axis=-1, keepdims=True) + EPS)
    n = x * r * g_ref[...]
    o_ref[...] = n
    logits = jnp.dot(n.astype(BF16), wr_ref[...], preferred_element_type=F32)
    lane = lax.broadcasted_iota(jnp.int32, logits.shape, 1)
    logits = jnp.where(lane < N_EXPERTS, logits, NEG_BIG)
    v1 = jnp.max(logits, axis=-1, keepdims=True)
    i1 = jnp.min(jnp.where(logits == v1, lane, LANES), axis=-1, keepdims=True)
    rest = jnp.where(lane == i1, NEG_BIG, logits)
    v2 = jnp.max(rest, axis=-1, keepdims=True)
    i2 = jnp.min(jnp.where(rest == v2, lane, LANES), axis=-1, keepdims=True)
    e2 = jnp.exp(v2 - v1)
    w1 = 1.0 / (1.0 + e2)
    w2 = e2 / (1.0 + e2)
    idx_ref[...] = jnp.where(lane == 0, i1, i2)
    wts_ref[...] = jnp.where(lane == 0, w1, w2)


def rmsnorm_router(x, g, w_router_pad, tm=512):
    n, d = x.shape
    return pl.pallas_call(
        _rmsnorm_router_kernel,
        out_shape=(jax.ShapeDtypeStruct((n, d), F32),
                   jax.ShapeDtypeStruct((n, LANES), jnp.int32),
                   jax.ShapeDtypeStruct((n, LANES), F32)),
        grid=(n // tm,),
        in_specs=[pl.BlockSpec((tm, d), lambda i: (i, 0)),
                  pl.BlockSpec((1, d), lambda i: (0, 0)),
                  pl.BlockSpec((d, LANES), lambda i: (0, 0))],
        out_specs=(pl.BlockSpec((tm, d), lambda i: (i, 0)),
                   pl.BlockSpec((tm, LANES), lambda i: (i, 0)),
                   pl.BlockSpec((tm, LANES), lambda i: (i, 0))),
        compiler_params=_params("parallel"),
        name="rmsnorm_router",
    )(x, g.reshape(1, d), w_router_pad)


def _proj_kernel(a_ref, w_ref, o_ref):
    o_ref[...] = jnp.dot(a_ref[...], w_ref[...], preferred_element_type=F32).astype(o_ref.dtype)


def in_proj(a, w, tm=MM_TILE, tn=MM_TILE):
    m, k = a.shape
    n = w.shape[1]
    return pl.pallas_call(
        _proj_kernel,
        out_shape=jax.ShapeDtypeStruct((m, n), BF16),
        grid=(m // tm, n // tn),
        in_specs=[pl.BlockSpec((tm, k), lambda i, j: (i, 0)),
                  pl.BlockSpec((k, tn), lambda i, j: (0, j))],
        out_specs=pl.BlockSpec((tm, tn), lambda i, j: (i, j)),
        compiler_params=_params("parallel", "parallel"),
        name="in_proj",
    )(a, w)


def _out_proj_kernel(a1_ref, a2_ref, w1_ref, w2_ref, res_ref, o_ref):
    acc = jnp.dot(a1_ref[...], w1_ref[...], preferred_element_type=F32)
    acc += jnp.dot(a2_ref[...], w2_ref[...], preferred_element_type=F32)
    o_ref[...] = res_ref[...] + acc


def out_proj(a1, a2, w, res, tm=MM_TILE, tn=MM_TILE):
    m, k1 = a1.shape
    assert a2.shape[1] == k1 and w.shape[0] == 2 * k1
    n = w.shape[1]
    return pl.pallas_call(
        _out_proj_kernel,
        out_shape=jax.ShapeDtypeStruct((m, n), F32),
        grid=(m // tm, n // tn),
        in_specs=[pl.BlockSpec((tm, k1), lambda i, j: (i, 0)),
                  pl.BlockSpec((tm, k1), lambda i, j: (i, 0)),
                  pl.BlockSpec((k1, tn), lambda i, j: (0, j)),
                  pl.BlockSpec((k1, tn), lambda i, j: (1, j)),
                  pl.BlockSpec((tm, tn), lambda i, j: (i, j))],
        out_specs=pl.BlockSpec((tm, tn), lambda i, j: (i, j)),
        compiler_params=_params("parallel", "parallel"),
        name="out_proj",
    )(a1, a2, w, w, res)


def _silu(x):
    return x * (1.0 / (1.0 + jnp.exp(-x)))


def _ffn_up_kernel(n_ref, wg_ref, wu_ref, o_ref):
    a = n_ref[...]
    g = jnp.dot(a, wg_ref[...], preferred_element_type=F32)
    u = jnp.dot(a, wu_ref[...], preferred_element_type=F32)
    o_ref[...] = (_silu(g) * u).astype(o_ref.dtype)


def ffn_up(n, wg, wu, tm=MM_TILE, tn=512):
    m, k = n.shape
    f = wg.shape[1]
    return pl.pallas_call(
        _ffn_up_kernel,
        out_shape=jax.ShapeDtypeStruct((m, f), BF16),
        grid=(m // tm, f // tn),
        in_specs=[pl.BlockSpec((tm, k), lambda i, j: (i, 0)),
                  pl.BlockSpec((k, tn), lambda i, j: (0, j)),
                  pl.BlockSpec((k, tn), lambda i, j: (0, j))],
        out_specs=pl.BlockSpec((tm, tn), lambda i, j: (i, j)),
        compiler_params=_params("parallel", "parallel"),
        name="ffn_up",
    )(n, wg, wu)


MOE_TILE = 512
GATHER_ROWS = 512


def _route_plan(idx, n_tiles):
    n = idx.shape[0]
    ea = jnp.concatenate([idx[:, 0], idx[:, 1]])
    onehot = (ea[:, None] == jnp.arange(N_EXPERTS, dtype=jnp.int32)[None, :]).astype(jnp.int32)
    incl = jnp.cumsum(onehot, axis=0)
    counts = incl[-1]
    padded = ((counts + MOE_TILE - 1) // MOE_TILE) * MOE_TILE
    ends = jnp.cumsum(padded)
    starts = ends - padded
    pos = jnp.sum(onehot * (starts[None, :] + incl - 1), axis=1)
    token = jnp.arange(2 * n, dtype=jnp.int32) % n
    row_token = jnp.zeros((n_tiles * MOE_TILE,), jnp.int32).at[pos].set(token)
    tile_start = jnp.arange(n_tiles, dtype=jnp.int32) * MOE_TILE
    tile_expert = jnp.minimum(jnp.sum((tile_start[:, None] >= ends[None, :]).astype(jnp.int32), axis=1),
                              N_EXPERTS - 1)
    n_valid = (ends[-1:] // MOE_TILE).astype(jnp.int32)
    return row_token, pos.astype(jnp.int32), tile_expert.astype(jnp.int32), n_valid


def _row_dmas(idx_ref, src_ref, buf_ref, sem, count, buf_base, wait):
    def body(i, carry):
        copy = pltpu.make_async_copy(src_ref.at[pl.ds(idx_ref[0, 0, i], 1)],
                                     buf_ref.at[pl.ds(buf_base + i, 1)], sem)
        if wait:
            copy.wait()
        else:
            copy.start()
        return carry

    lax.fori_loop(0, count, body, 0, unroll=8)


def _prefetched_rows(step_dmas):
    s = pl.program_id(0)
    slot = s % 2

    @pl.when(s == 0)
    def _():
        step_dmas(False, 0, False)

    @pl.when(s + 1 < pl.num_programs(0))
    def _():
        step_dmas(True, 1 - slot, False)

    step_dmas(False, slot, True)
    return slot


def _gather_cast_kernel(idx_ref, nxt_ref, src_ref, dst_ref, buf_ref, sem):
    def step_dmas(nxt, slot, wait):
        _row_dmas(nxt_ref if nxt else idx_ref, src_ref, buf_ref.at[slot], sem.at[slot], GATHER_ROWS, 0, wait)

    slot = _prefetched_rows(step_dmas)
    dst_ref[...] = buf_ref[slot].astype(dst_ref.dtype)


def gather_rows_bf16(src, idx):
    r = idx.shape[0]
    d = src.shape[1]
    steps = r // GATHER_ROWS
    idx_blocks = idx.reshape(steps, 1, GATHER_ROWS)
    smem_block = functools.partial(pl.BlockSpec, (1, 1, GATHER_ROWS), memory_space=pltpu.SMEM)
    return pl.pallas_call(
        _gather_cast_kernel,
        out_shape=jax.ShapeDtypeStruct((r, d), BF16),
        grid=(steps,),
        in_specs=[smem_block(lambda i: (i, 0, 0)),
                  smem_block(lambda i: (jnp.minimum(i + 1, steps - 1), 0, 0)),
                  pl.BlockSpec(memory_space=pl.ANY)],
        out_specs=pl.BlockSpec((GATHER_ROWS, d), lambda i: (i, 0)),
        scratch_shapes=[pltpu.VMEM((2, GATHER_ROWS, d), src.dtype), pltpu.SemaphoreType.DMA((2,))],
        compiler_params=_params("arbitrary"),
        name="gather_rows",
    )(idx_blocks, idx_blocks, src)


def _moe_up_kernel(te_ref, nv_ref, x_ref, wg_ref, wu_ref, o_ref):
    valid = pl.program_id(1) < nv_ref[0]

    @pl.when(valid)
    def _():
        a = x_ref[...]
        g = jnp.dot(a, wg_ref[...], preferred_element_type=F32)
        u = jnp.dot(a, wu_ref[...], preferred_element_type=F32)
        o_ref[...] = (_silu(g) * u).astype(o_ref.dtype)

    @pl.when(jnp.logical_not(valid))
    def _():
        o_ref[...] = jnp.zeros_like(o_ref)


def moe_up(x_rows, tile_expert, n_valid, wg, wu, tn=1024):
    r, k = x_rows.shape
    f = wg.shape[2]
    tm = MOE_TILE
    w_spec = pl.BlockSpec((None, k, tn), lambda j, m, te, nv: (te[m], 0, j))
    return pl.pallas_call(
        _moe_up_kernel,
        out_shape=jax.ShapeDtypeStruct((r, f), BF16),
        grid_spec=pltpu.PrefetchScalarGridSpec(
            num_scalar_prefetch=2,
            grid=(f // tn, r // tm),
            in_specs=[pl.BlockSpec((tm, k), lambda j, m, te, nv: (m, 0)), w_spec, w_spec],
            out_specs=pl.BlockSpec((tm, tn), lambda j, m, te, nv: (m, j))),
        compiler_params=_params("parallel", "arbitrary"),
        name="moe_up",
    )(tile_expert, n_valid, x_rows, wg, wu)


def _moe_down_kernel(te_ref, nv_ref, a_ref, w_ref, o_ref):
    valid = pl.program_id(1) < nv_ref[0]

    @pl.when(valid)
    def _():
        o_ref[...] = jnp.dot(a_ref[...], w_ref[...], preferred_element_type=F32)

    @pl.when(jnp.logical_not(valid))
    def _():
        o_ref[...] = jnp.zeros_like(o_ref)


def moe_down(act_rows, tile_expert, n_valid, wd, tn=2048):
    r, f = act_rows.shape
    d = wd.shape[2]
    tm = MOE_TILE
    return pl.pallas_call(
        _moe_down_kernel,
        out_shape=jax.ShapeDtypeStruct((r, d), F32),
        grid_spec=pltpu.PrefetchScalarGridSpec(
            num_scalar_prefetch=2,
            grid=(d // tn, r // tm),
            in_specs=[pl.BlockSpec((tm, f), lambda j, m, te, nv: (m, 0)),
                      pl.BlockSpec((None, f, tn), lambda j, m, te, nv: (te[m], 0, j))],
            out_specs=pl.BlockSpec((tm, tn), lambda j, m, te, nv: (m, j))),
        compiler_params=_params("parallel", "arbitrary"),
        name="moe_down",
    )(tile_expert, n_valid, act_rows, wd)


def _moe_combine_kernel(pos1_ref, pos2_ref, nxt1_ref, nxt2_ref, h_ref, y_ref, wts_ref, g_ref, *refs, keep_x):
    out_refs, (ybuf_ref, sem) = refs[:-2], refs[-2:]
    tm = h_ref.shape[0]

    def step_dmas(nxt, slot, wait):
        first, second = (nxt1_ref, nxt2_ref) if nxt else (pos1_ref, pos2_ref)
        _row_dmas(first, y_ref, ybuf_ref.at[slot], sem.at[slot], tm, 0, wait)
        _row_dmas(second, y_ref, ybuf_ref.at[slot], sem.at[slot], tm, tm, wait)

    slot = _prefetched_rows(step_dmas)
    w = wts_ref[...]
    x = h_ref[...] + w[:, 0:1] * ybuf_ref[slot, 0:tm, :] + w[:, 1:2] * ybuf_ref[slot, tm:2 * tm, :]
    if keep_x:
        out_refs[0][...] = x
    r = lax.rsqrt(jnp.mean(x * x, axis=-1, keepdims=True) + EPS)
    out_refs[-1][...] = (x * r * g_ref[...]).astype(out_refs[-1].dtype)


def moe_combine(h, y_rows, pos, wts, next_norm_g, normed_dtype, keep_x, tm=256):
    n, d = h.shape
    nt = n // tm
    row_spec = pl.BlockSpec((tm, d), lambda i: (i, 0))
    shapes = [jax.ShapeDtypeStruct((n, d), F32)] * keep_x + [jax.ShapeDtypeStruct((n, d), normed_dtype)]
    pos_blocks = pos.reshape(2 * nt, 1, tm)
    smem_block = functools.partial(pl.BlockSpec, (1, 1, tm), memory_space=pltpu.SMEM)
    nxt = lambda i: jnp.minimum(i + 1, nt - 1)
    return pl.pallas_call(
        functools.partial(_moe_combine_kernel, keep_x=keep_x),
        out_shape=tuple(shapes),
        grid=(nt,),
        in_specs=[smem_block(lambda i: (i, 0, 0)),
                  smem_block(lambda i: (i + nt, 0, 0)),
                  smem_block(lambda i: (nxt(i), 0, 0)),
                  smem_block(lambda i: (nxt(i) + nt, 0, 0)),
                  row_spec,
                  pl.BlockSpec(memory_space=pl.ANY),
                  pl.BlockSpec((tm, LANES), lambda i: (i, 0)),
                  pl.BlockSpec((1, d), lambda i: (0, 0))],
        out_specs=tuple([row_spec] * len(shapes)),
        scratch_shapes=[pltpu.VMEM((2, 2 * tm, d), y_rows.dtype), pltpu.SemaphoreType.DMA((2,))],
        compiler_params=_params("arbitrary"),
        name="moe_combine",
    )(pos_blocks, pos_blocks, pos_blocks, pos_blocks, h, y_rows, wts, next_norm_g.reshape(1, d))


def moe_ffn(h, norm_g, w_router_pad, wg, wu, wd, next_norm_g, normed_dtype, keep_x):
    n = h.shape[0]
    n_tiles = (2 * n) // MOE_TILE + N_EXPERTS
    n2, idx, wts = rmsnorm_router(h, norm_g, w_router_pad)
    row_token, pos, tile_expert, n_valid = _route_plan(idx, n_tiles)
    x_rows = gather_rows_bf16(n2, row_token)
    act_rows = moe_up(x_rows, tile_expert, n_valid, wg, wu)
    y_rows = moe_down(act_rows, tile_expert, n_valid, wd)
    return moe_combine(h, y_rows, pos, wts, next_norm_g, normed_dtype, keep_x)


def _ffn_down_kernel(a_ref, w_ref, res_ref, o_ref):
    @pl.when(pl.program_id(2) == 0)
    def _():
        o_ref[...] = res_ref[...]

    o_ref[...] += jnp.dot(a_ref[...], w_ref[...], preferred_element_type=F32)


def ffn_down(a, w, res, tm=MM_TILE, tn=MM_TILE, tk=4096):
    m, k = a.shape
    n = w.shape[1]
    return pl.pallas_call(
        _ffn_down_kernel,
        out_shape=jax.ShapeDtypeStruct((m, n), F32),
        grid=(m // tm, n // tn, k // tk),
        in_specs=[pl.BlockSpec((tm, tk), lambda i, j, kk: (i, kk)),
                  pl.BlockSpec((tk, tn), lambda i, j, kk: (kk, j)),
                  pl.BlockSpec((tm, tn), lambda i, j, kk: (i, j))],
        out_specs=pl.BlockSpec((tm, tn), lambda i, j, kk: (i, j)),
        compiler_params=_params("parallel", "parallel", "arbitrary"),
        name="ffn_down",
    )(a, w, res)


def _alibi_slope(h):
    return float(2.0 ** (-8.0 * (h + 1) / SWA_Q_HEADS))


def _swa_bias_table():
    qi = np.arange(WINDOW)[:, None]
    kj = np.arange(2 * WINDOW)[None, :]
    dist = qi - kj + WINDOW
    inside = (dist >= 0) & (dist < WINDOW)
    slopes = np.array([_alibi_slope(h) for h in range(SWA_Q_HEADS)], np.float64)
    bias = -slopes[:, None, None] * dist[None].astype(np.float64)
    later = np.where(inside[None], bias, NEG_BIG)
    first = np.where((inside & (kj >= WINDOW))[None], bias, NEG_BIG)
    table = np.stack([first, later]).astype(np.float32)
    return table.reshape(2, SWA_KV_HEADS, SWA_GROUP * WINDOW, 2 * WINDOW)


def _swa_kernel(q_ref, kp_ref, kc_ref, vp_ref, vc_ref, bias_ref, sink_ref, g_ref, o_ref, acc_ref):
    nt = (((1,), (1,)), ((), ()))
    zeros_half = jnp.zeros((2 * WINDOW, SWA_HEAD_DIM), BF16)
    ones = jnp.ones((2 * WINDOW, LANES), BF16)
    pairs = SWA_GROUP // 2

    def padded(ref_prev, ref_cur, j):
        cols = slice(j * SWA_HEAD_DIM, (j + 1) * SWA_HEAD_DIM)
        x = jnp.concatenate([ref_prev[:, cols], ref_cur[:, cols]], axis=0)
        return jnp.concatenate([x, zeros_half], axis=1), jnp.concatenate([zeros_half, x], axis=1)

    scores, maxes = [], []
    for j in range(SWA_KV_HEADS):
        k_pads = padded(kp_ref, kc_ref, j)
        parts = []
        for p in range(pairs):
            lanes = slice((j * pairs + p) * LANES, (j * pairs + p + 1) * LANES)
            q_pair = q_ref[:, lanes] * jnp.asarray(SWA_HEAD_DIM ** -0.5, BF16)
            for k_pad in k_pads:
                parts.append(lax.dot_general(q_pair, k_pad, nt, preferred_element_type=F32))
        s = jnp.concatenate(parts, axis=0) + bias_ref[j]
        scores.append(s)
        maxes.append(jnp.maximum(jnp.max(s, axis=-1, keepdims=True), sink_ref[j]))

    probs, invs = [], []
    for j in range(SWA_KV_HEADS):
        m = maxes[j]
        e = jnp.exp(scores[j] - jnp.concatenate([m, m], axis=1)).astype(BF16)
        denom = jnp.dot(e, ones, preferred_element_type=F32) + jnp.exp(sink_ref[j] - m)
        probs.append(e)
        invs.append(1.0 / denom)

    for j in range(SWA_KV_HEADS):
        v_pads = padded(vp_ref, vc_ref, j)
        for p in range(pairs):
            lanes = slice((j * pairs + p) * LANES, (j * pairs + p + 1) * LANES)
            out_pair = None
            for half in range(2):
                rows = slice((2 * p + half) * WINDOW, (2 * p + half + 1) * WINDOW)
                o = jnp.dot(probs[j][rows], v_pads[half], preferred_element_type=F32) * invs[j][rows]
                out_pair = o if out_pair is None else out_pair + o
            acc_ref[:, lanes] = out_pair

    acc = acc_ref[...]
    r = lax.rsqrt(jnp.mean(acc * acc, axis=-1, keepdims=True) + EPS)
    o_ref[...] = (acc * r * g_ref[...]).astype(o_ref.dtype)


def swa_attention(proj, sinks, norm_g, batch, seq):
    nb = seq // WINDOW
    rows = lambda b, n: b * nb + n
    prev = lambda b, n: b * nb + jnp.maximum(n - 1, 0)
    kcol = COL_SK // SWA_KV_WIDTH
    vcol = COL_SV // SWA_KV_WIDTH
    gq = SWA_GROUP * WINDOW
    bias = jnp.asarray(_swa_bias_table())
    sink_rows = jnp.broadcast_to(jnp.repeat(sinks.astype(F32), WINDOW).reshape(SWA_KV_HEADS, gq, 1),
                                 (SWA_KV_HEADS, gq, LANES))
    return pl.pallas_call(
        _swa_kernel,
        out_shape=jax.ShapeDtypeStruct((batch * seq, SWA_WIDTH), BF16),
        grid=(batch, nb),
        in_specs=[pl.BlockSpec((WINDOW, SWA_WIDTH), lambda b, n: (rows(b, n), COL_SQ // SWA_WIDTH)),
                  pl.BlockSpec((WINDOW, SWA_KV_WIDTH), lambda b, n: (prev(b, n), kcol)),
                  pl.BlockSpec((WINDOW, SWA_KV_WIDTH), lambda b, n: (rows(b, n), kcol)),
                  pl.BlockSpec((WINDOW, SWA_KV_WIDTH), lambda b, n: (prev(b, n), vcol)),
                  pl.BlockSpec((WINDOW, SWA_KV_WIDTH), lambda b, n: (rows(b, n), vcol)),
                  pl.BlockSpec((None, SWA_KV_HEADS, gq, 2 * WINDOW),
                               lambda b, n: (jnp.minimum(n, 1), 0, 0, 0)),
                  pl.BlockSpec((SWA_KV_HEADS, gq, LANES), lambda b, n: (0, 0, 0)),
                  pl.BlockSpec((1, SWA_WIDTH), lambda b, n: (0, 0))],
        out_specs=pl.BlockSpec((WINDOW, SWA_WIDTH), lambda b, n: (rows(b, n), 0)),
        scratch_shapes=[pltpu.VMEM((WINDOW, SWA_WIDTH), F32)],
        compiler_params=_params("parallel", "parallel"),
        name="swa_attention",
    )(proj, proj, proj, proj, proj, bias, sink_rows, norm_g.reshape(1, SWA_WIDTH))


GLA_STEP_CHUNKS = 4


def _gla_kernel(q_ref, k_ref, v_ref, gg_ref, glr_ref, gw_ref, gb_ref, ng_ref, o_ref, st_ref):
    c = GLA_CHUNK
    nt = (((1,), (1,)), ((), ()))

    @pl.when(pl.program_id(1) == 0)
    def _():
        st_ref[...] = jnp.zeros_like(st_ref)

    ri = lax.broadcasted_iota(jnp.int32, (c, c), 0)
    ci = lax.broadcasted_iota(jnp.int32, (c, c), 1)
    causal = ri >= ci
    tri = jnp.where(causal, 1.0, 0.0).astype(BF16)

    scaled = []
    for n in range(GLA_STEP_CHUNKS):
        rows = slice(n * c, (n + 1) * c)
        logits = jnp.dot(glr_ref[rows, :], gw_ref[...], preferred_element_type=F32) + gb_ref[...]
        log_a = (jnp.minimum(logits, 0.0) - jnp.log(1.0 + jnp.exp(-jnp.abs(logits)))) * (1.0 / GLA_TAU)
        la_hi = log_a.astype(BF16)
        la_lo = (log_a - la_hi.astype(F32)).astype(BF16)
        b = (jnp.dot(tri, la_hi, preferred_element_type=F32)
             + jnp.dot(tri, la_lo, preferred_element_type=F32))
        b_last = b[c - 1:c, :]
        q = q_ref[rows, :].astype(F32)
        k = k_ref[rows, :].astype(F32)
        q_dec = (q * jnp.exp(b) * (GLA_DK ** -0.5)).astype(BF16)
        k_inv = (k * jnp.exp(-b)).astype(BF16)
        k_end = (k * jnp.exp(b_last - b)).astype(BF16)
        scaled.append((q_dec, k_inv, k_end, jnp.exp(b_last)))

    for h in range(GLA_HEADS):
        kc = slice(h * GLA_DK, (h + 1) * GLA_DK)
        vc = slice(h * GLA_DV, (h + 1) * GLA_DV)
        st = st_ref[h]
        for n in range(GLA_STEP_CHUNKS):
            rows = slice(n * c, (n + 1) * c)
            q_dec, k_inv, k_end, dec = scaled[n]
            v = v_ref[rows, vc]
            a = lax.dot_general(q_dec[:, kc], k_inv[:, kc], nt, preferred_element_type=F32)
            a = jnp.where(causal, a, 0.0).astype(BF16)
            o = jnp.dot(a, v, preferred_element_type=F32)
            o += lax.dot_general(q_dec[:, kc], st.astype(BF16), nt, preferred_element_type=F32)
            upd = lax.dot_general(v, k_end[:, kc], (((0,), (0,)), ((), ())),
                                  preferred_element_type=F32)
            st = st * dec[:, kc] + upd
            r = lax.rsqrt(jnp.mean(o * o, axis=-1, keepdims=True) + EPS)
            gate = _silu(gg_ref[rows, vc].astype(F32))
            o_ref[rows, vc] = (o * r * ng_ref[...] * gate).astype(o_ref.dtype)
        st_ref[h] = st


def gla_attention(proj, gate_w_pad, gate_b, norm_g, batch, seq):
    c = GLA_CHUNK * GLA_STEP_CHUNKS
    nc = seq // c
    rows = lambda b, n: b * nc + n
    return pl.pallas_call(
        _gla_kernel,
        out_shape=jax.ShapeDtypeStruct((batch * seq, GLA_WIDTH), BF16),
        grid=(batch, nc),
        in_specs=[pl.BlockSpec((c, GLA_KEY_WIDTH), lambda b, n: (rows(b, n), COL_GQ // GLA_KEY_WIDTH)),
                  pl.BlockSpec((c, GLA_KEY_WIDTH), lambda b, n: (rows(b, n), COL_GK // GLA_KEY_WIDTH)),
                  pl.BlockSpec((c, GLA_WIDTH), lambda b, n: (rows(b, n), COL_GV // GLA_WIDTH)),
                  pl.BlockSpec((c, GLA_WIDTH), lambda b, n: (rows(b, n), COL_GG // GLA_WIDTH)),
                  pl.BlockSpec((c, LANES), lambda b, n: (rows(b, n), COL_GLR // LANES)),
                  pl.BlockSpec((LANES, GLA_KEY_WIDTH), lambda b, n: (0, 0)),
                  pl.BlockSpec((1, GLA_KEY_WIDTH), lambda b, n: (0, 0)),
                  pl.BlockSpec((1, GLA_DV), lambda b, n: (0, 0))],
        out_specs=pl.BlockSpec((c, GLA_WIDTH), lambda b, n: (rows(b, n), 0)),
        scratch_shapes=[pltpu.VMEM((GLA_HEADS, GLA_DV, GLA_DK), F32)],
        compiler_params=_params("parallel", "arbitrary"),
        name="gla_attention",
    )(proj, proj, proj, proj, proj, gate_w_pad, gate_b.reshape(1, GLA_KEY_WIDTH),
      norm_g.reshape(1, GLA_DV))


CAST_BLOCK_BYTES = 8 * 1024 * 1024


def _cast_kernel(w_ref, o_ref):
    o_ref[...] = w_ref[...].astype(o_ref.dtype)


def cast_weights(w, first, count):
    _, r, c = w.shape
    tr = max(8, min(r, CAST_BLOCK_BYTES // (4 * c)))
    return pl.pallas_call(
        _cast_kernel,
        out_shape=jax.ShapeDtypeStruct((count, r, c), BF16),
        grid=(count, r // tr),
        in_specs=[pl.BlockSpec((None, tr, c), lambda e, i: (first + e, i, 0))],
        out_specs=pl.BlockSpec((None, tr, c), lambda e, i: (e, i, 0)),
        compiler_params=_params("parallel", "parallel"),
        name="cast_weights",
    )(w)


_W_IN_SEGMENTS = ((0, 2048, COL_SQ), (2048, 2304, COL_SK), (2304, 2560, COL_SV), (2560, 3584, COL_GQ),
                  (3584, 4608, COL_GK), (4608, 6656, COL_GV), (6656, 8704, COL_GG))
IN_COLS = 8720
PREP_COLS = 256


def _w_in_source_blocks():
    table = np.zeros((PROJ_COLS // PREP_COLS,), np.int32)
    for lo, hi, dst in _W_IN_SEGMENTS:
        for k in range((hi - lo) // PREP_COLS):
            table[dst // PREP_COLS + k] = lo // PREP_COLS + k
    return table


def _prep_w_in_kernel(tbl_ref, wt_ref, tail_ref, o_ref):
    jb = pl.program_id(0)
    full_blocks = COL_GLR // PREP_COLS

    @pl.when(jb < full_blocks)
    def _():
        o_ref[...] = wt_ref[...].T.astype(o_ref.dtype)

    @pl.when(jb == full_blocks)
    def _():
        o_ref[:, 0:LANES] = tail_ref[...].T.astype(o_ref.dtype)
        o_ref[:, LANES:PREP_COLS] = jnp.zeros((o_ref.shape[0], PREP_COLS - LANES), o_ref.dtype)

    @pl.when(jb > full_blocks)
    def _():
        o_ref[...] = jnp.zeros_like(o_ref)


def prep_w_in(w_in, layer):
    _, d, _ = w_in.shape
    w_t = jnp.transpose(w_in, (0, 2, 1))
    tail = jnp.pad(w_t[layer, COL_GLR:IN_COLS], ((0, LANES - GLA_GATE_RANK), (0, 0)))
    return pl.pallas_call(
        _prep_w_in_kernel,
        out_shape=jax.ShapeDtypeStruct((d, PROJ_COLS), BF16),
        grid_spec=pltpu.PrefetchScalarGridSpec(
            num_scalar_prefetch=1,
            grid=(PROJ_COLS // PREP_COLS,),
            in_specs=[pl.BlockSpec((None, PREP_COLS, d), lambda j, tbl: (layer, tbl[j], 0)),
                      pl.BlockSpec((LANES, d), lambda j, tbl: (0, 0))],
            out_specs=pl.BlockSpec((d, PREP_COLS), lambda j, tbl: (0, j))),
        compiler_params=_params("arbitrary"),
        name="prep_w_in",
    )(jnp.asarray(_w_in_source_blocks()), w_t, tail)


def kernel(x, attn_norm_g, w_in, gla_gate_w, gla_gate_b, gla_norm_g, swa_sinks, swa_norm_g, w_out,
           ffn_norm_g, dense_w_gate, dense_w_up, dense_w_down, moe_router, moe_w_gate, moe_w_up,
           moe_w_down, final_norm_g):
    batch, seq, d = x.shape
    xs = x.reshape(batch * seq, d)
    moe_shape = (DEPTH // 2 * N_EXPERTS, d, D_FF_EXPERT)
    n = rmsnorm(xs, attn_norm_g[0], BF16)
    for layer in range(DEPTH):
        proj = in_proj(n, prep_w_in(w_in, layer))
        swa = swa_attention(proj, swa_sinks[layer], swa_norm_g[layer], batch, seq)
        gate_w_pad = jnp.pad(gla_gate_w[layer], ((0, LANES - GLA_GATE_RANK), (0, 0))).astype(BF16)
        gla = gla_attention(proj, gate_w_pad, gla_gate_b[layer], gla_norm_g[layer], batch, seq)
        h = out_proj(swa, gla, cast_weights(w_out, layer, 1)[0], xs)
        j = layer // 2
        if layer % 2 == 0:
            n2 = rmsnorm(h, ffn_norm_g[layer], BF16)
            act = ffn_up(n2, cast_weights(dense_w_gate, j, 1)[0], cast_weights(dense_w_up, j, 1)[0])
            xs = ffn_down(act, cast_weights(dense_w_down, j, 1)[0], h)
            n = rmsnorm(xs, attn_norm_g[layer + 1], BF16)
        else:
            wr = jnp.pad(moe_router[j], ((0, 0), (0, LANES - N_EXPERTS))).astype(BF16)
            wg = cast_weights(moe_w_gate.reshape(moe_shape), j * N_EXPERTS, N_EXPERTS)
            wu = cast_weights(moe_w_up.reshape(moe_shape), j * N_EXPERTS, N_EXPERTS)
            wd = cast_weights(moe_w_down.reshape(moe_shape[0], D_FF_EXPERT, d), j * N_EXPERTS, N_EXPERTS)
            if layer + 1 < DEPTH:
                xs, n = moe_ffn(h, ffn_norm_g[layer], wr, wg, wu, wd, attn_norm_g[layer + 1], BF16, True)
            else:
                (out,) = moe_ffn(h, ffn_norm_g[layer], wr, wg, wu, wd, final_norm_g, F32, False)
    return out.reshape(batch, seq, d)
```

```python
import functools

import jax
import jax.numpy as jnp
import numpy as np
from jax import lax
from jax.experimental import pallas as pl
from jax.experimental.pallas import tpu as pltpu

BF16 = jnp.bfloat16
F32 = jnp.float32

D_MODEL = 4096
DEPTH = 4
EPS = 1e-5

SWA_WIDTH = 2048
SWA_HEAD_DIM = 64
SWA_Q_HEADS = 32
SWA_KV_HEADS = 4
SWA_GROUP = 8
SWA_KV_WIDTH = SWA_KV_HEADS * SWA_HEAD_DIM
WINDOW = 128

GLA_WIDTH = 2048
GLA_HEADS = 4
GLA_KEY_WIDTH = 1024
GLA_DK = 256
GLA_DV = 512
GLA_GATE_RANK = 16
GLA_TAU = 16.0
GLA_CHUNK = 64

D_FF_DENSE = 8192
N_EXPERTS = 8
D_FF_EXPERT = 2048

LANES = 128
VMEM_LIMIT = 56 * 1024 * 1024

COL_SQ = 0
COL_GV = 2048
COL_GG = 4096
COL_GQ = 6144
COL_GK = 7168
COL_SK = 8192
COL_SV = 8448
COL_GLR = 8704
PROJ_COLS = 8960
PROJ_TILE = 1280

MM_TILE = 1024
NEG_BIG = -1e30


def _params(*sem):
    return pltpu.CompilerParams(dimension_semantics=sem, vmem_limit_bytes=VMEM_LIMIT)


def _rmsnorm_kernel(x_ref, g_ref, o_ref):
    x = x_ref[...]
    r = lax.rsqrt(jnp.mean(x * x, axis=-1, keepdims=True) + EPS)
    o_ref[...] = (x * r * g_ref[...]).astype(o_ref.dtype)


def rmsnorm(x, g, out_dtype, tm=512):
    n, d = x.shape
    return pl.pallas_call(
        _rmsnorm_kernel,
        out_shape=jax.ShapeDtypeStruct((n, d), out_dtype),
        grid=(n // tm,),
        in_specs=[pl.BlockSpec((tm, d), lambda i: (i, 0)),
                  pl.BlockSpec((1, d), lambda i: (0, 0))],
        out_specs=pl.BlockSpec((tm, d), lambda i: (i, 0)),
        compiler_params=_params("parallel"),
        name="rmsnorm",
    )(x, g.reshape(1, d))


def _pack_halves(x):
    w = x.shape[1] // 2
    bits = lax.bitcast_convert_type(x.astype(BF16).astype(F32), jnp.uint32)
    return (bits[:, :w] >> 16) | (bits[:, w:] & jnp.uint32(0xFFFF0000))


def _unpack_halves(p):
    lo = lax.bitcast_convert_type(p << 16, F32)
    hi = lax.bitcast_convert_type(p & jnp.uint32(0xFFFF0000), F32)
    return lo, hi


def _rmsnorm_router_kernel(x_ref, g_ref, wr_ref, o_ref, idx_ref, wts_ref):
    x = x_ref[...]
    r = lax.rsqrt(jnp.mean(x * x, axis=-1, keepdims=True) + EPS)
    n = x * r * g_ref[...]
    o_ref[...] = _pack_halves(n)
    logits = jnp.dot(n.astype(BF16), wr_ref[...], preferred_element_type=F32)
    lane = lax.broadcasted_iota(jnp.int32, logits.shape, 1)
    logits = jnp.where(lane < N_EXPERTS, logits, NEG_BIG)
    v1 = jnp.max(logits, axis=-1, keepdims=True)
    i1 = jnp.min(jnp.where(logits == v1, lane, LANES), axis=-1, keepdims=True)
    rest = jnp.where(lane == i1, NEG_BIG, logits)
    v2 = jnp.max(rest, axis=-1, keepdims=True)
    i2 = jnp.min(jnp.where(rest == v2, lane, LANES), axis=-1, keepdims=True)
    e2 = jnp.exp(v2 - v1)
    w1 = 1.0 / (1.0 + e2)
    w2 = e2 / (1.0 + e2)
    idx_ref[...] = jnp.where(lane == 0, i1, i2)
    wts_ref[...] = jnp.where(lane == 0, w1, w2)


def rmsnorm_router(x, g, w_router_pad, tm=512):
    n, d = x.shape
    return pl.pallas_call(
        _rmsnorm_router_kernel,
        out_shape=(jax.ShapeDtypeStruct((n, d // 2), jnp.uint32),
                   jax.ShapeDtypeStruct((n, LANES), jnp.int32),
                   jax.ShapeDtypeStruct((n, LANES), F32)),
        grid=(n // tm,),
        in_specs=[pl.BlockSpec((tm, d), lambda i: (i, 0)),
                  pl.BlockSpec((1, d), lambda i: (0, 0)),
                  pl.BlockSpec((d, LANES), lambda i: (0, 0))],
        out_specs=(pl.BlockSpec((tm, d // 2), lambda i: (i, 0)),
                   pl.BlockSpec((tm, LANES), lambda i: (i, 0)),
                   pl.BlockSpec((tm, LANES), lambda i: (i, 0))),
        compiler_params=_params("parallel"),
        name="rmsnorm_router",
    )(x, g.reshape(1, d), w_router_pad)


def _proj_kernel(a_ref, w_ref, o_ref):
    o_ref[...] = jnp.dot(a_ref[...], w_ref[...], preferred_element_type=F32).astype(o_ref.dtype)


def in_proj(a, w, tm=MM_TILE, tn=PROJ_TILE):
    m, k = a.shape
    n = w.shape[1]
    return pl.pallas_call(
        _proj_kernel,
        out_shape=jax.ShapeDtypeStruct((m, n), BF16),
        grid=(m // tm, n // tn),
        in_specs=[pl.BlockSpec((tm, k), lambda i, j: (i, 0)),
                  pl.BlockSpec((k, tn), lambda i, j: (0, j))],
        out_specs=pl.BlockSpec((tm, tn), lambda i, j: (i, j)),
        compiler_params=_params("parallel", "parallel"),
        name="in_proj",
    )(a, w)


def _out_proj_kernel(a1_ref, a2_ref, w1_ref, w2_ref, res_ref, o_ref):
    acc = jnp.dot(a1_ref[...], w1_ref[...], preferred_element_type=F32)
    acc += jnp.dot(a2_ref[...], w2_ref[...], preferred_element_type=F32)
    o_ref[...] = res_ref[...] + acc


def out_proj(a1, a2, w, res, tm=MM_TILE, tn=MM_TILE):
    m, k1 = a1.shape
    assert a2.shape[1] == k1 and w.shape[0] == 2 * k1
    n = w.shape[1]
    return pl.pallas_call(
        _out_proj_kernel,
        out_shape=jax.ShapeDtypeStruct((m, n), F32),
        grid=(m // tm, n // tn),
        in_specs=[pl.BlockSpec((tm, k1), lambda i, j: (i, 0)),
                  pl.BlockSpec((tm, k1), lambda i, j: (i, 0)),
                  pl.BlockSpec((k1, tn), lambda i, j: (0, j)),
                  pl.BlockSpec((k1, tn), lambda i, j: (1, j)),
                  pl.BlockSpec((tm, tn), lambda i, j: (i, j))],
        out_specs=pl.BlockSpec((tm, tn), lambda i, j: (i, j)),
        compiler_params=_params("parallel", "parallel"),
        name="out_proj",
    )(a1, a2, w, w, res)


def _silu(x):
    return x * (1.0 / (1.0 + jnp.exp(-x)))


def _ffn_up_kernel(n_ref, wg_ref, wu_ref, o_ref):
    a = n_ref[...]
    g = jnp.dot(a, wg_ref[...], preferred_element_type=F32)
    u = jnp.dot(a, wu_ref[...], preferred_element_type=F32)
    o_ref[...] = (_silu(g) * u).astype(o_ref.dtype)


def ffn_up(n, wg, wu, tm=MM_TILE, tn=512):
    m, k = n.shape
    f = wg.shape[1]
    return pl.pallas_call(
        _ffn_up_kernel,
        out_shape=jax.ShapeDtypeStruct((m, f), BF16),
        grid=(m // tm, f // tn),
        in_specs=[pl.BlockSpec((tm, k), lambda i, j: (i, 0)),
                  pl.BlockSpec((k, tn), lambda i, j: (0, j)),
                  pl.BlockSpec((k, tn), lambda i, j: (0, j))],
        out_specs=pl.BlockSpec((tm, tn), lambda i, j: (i, j)),
        compiler_params=_params("parallel", "parallel"),
        name="ffn_up",
    )(n, wg, wu)


MOE_TILE = 512
GATHER_ROWS = 512


def _route_plan(idx, n_tiles):
    n = idx.shape[0]
    ea = jnp.concatenate([idx[:, 0], idx[:, 1]])
    onehot = (ea[:, None] == jnp.arange(N_EXPERTS, dtype=jnp.int32)[None, :]).astype(jnp.int32)
    incl = jnp.cumsum(onehot, axis=0)
    counts = incl[-1]
    padded = ((counts + MOE_TILE - 1) // MOE_TILE) * MOE_TILE
    ends = jnp.cumsum(padded)
    starts = ends - padded
    pos = jnp.sum(onehot * (starts[None, :] + incl - 1), axis=1)
    token = jnp.arange(2 * n, dtype=jnp.int32) % n
    row_token = jnp.zeros((n_tiles * MOE_TILE,), jnp.int32).at[pos].set(token)
    tile_start = jnp.arange(n_tiles, dtype=jnp.int32) * MOE_TILE
    tile_expert = jnp.minimum(jnp.sum((tile_start[:, None] >= ends[None, :]).astype(jnp.int32), axis=1),
                              N_EXPERTS - 1)
    n_valid = (ends[-1:] // MOE_TILE).astype(jnp.int32)
    return row_token, pos.astype(jnp.int32), tile_expert.astype(jnp.int32), n_valid


def _row_dmas(idx_ref, src_ref, buf_ref, sem, count, buf_base, wait):
    def body(i, carry):
        copy = pltpu.make_async_copy(src_ref.at[pl.ds(idx_ref[0, 0, i], 1)],
                                     buf_ref.at[pl.ds(buf_base + i, 1)], sem)
        if wait:
            copy.wait()
        else:
            copy.start()
        return carry

    lax.fori_loop(0, count, body, 0, unroll=8)


def _prefetched_rows(step_dmas):
    s = pl.program_id(0)
    slot = s % 2

    @pl.when(s == 0)
    def _():
        step_dmas(False, 0, False)

    @pl.when(s + 1 < pl.num_programs(0))
    def _():
        step_dmas(True, 1 - slot, False)

    step_dmas(False, slot, True)
    return slot


def _gather_cast_kernel(idx_ref, nxt_ref, src_ref, dst_ref, buf_ref, sem):
    def step_dmas(nxt, slot, wait):
        _row_dmas(nxt_ref if nxt else idx_ref, src_ref, buf_ref.at[slot], sem.at[slot], GATHER_ROWS, 0, wait)

    slot = _prefetched_rows(step_dmas)
    lo, hi = _unpack_halves(buf_ref[slot])
    w = lo.shape[1]
    dst_ref[:, :w] = lo.astype(dst_ref.dtype)
    dst_ref[:, w:] = hi.astype(dst_ref.dtype)


def gather_rows_bf16(src, idx):
    r = idx.shape[0]
    d = 2 * src.shape[1]
    steps = r // GATHER_ROWS
    idx_blocks = idx.reshape(steps, 1, GATHER_ROWS)
    smem_block = functools.partial(pl.BlockSpec, (1, 1, GATHER_ROWS), memory_space=pltpu.SMEM)
    return pl.pallas_call(
        _gather_cast_kernel,
        out_shape=jax.ShapeDtypeStruct((r, d), BF16),
        grid=(steps,),
        in_specs=[smem_block(lambda i: (i, 0, 0)),
                  smem_block(lambda i: (jnp.minimum(i + 1, steps - 1), 0, 0)),
                  pl.BlockSpec(memory_space=pl.ANY)],
        out_specs=pl.BlockSpec((GATHER_ROWS, d), lambda i: (i, 0)),
        scratch_shapes=[pltpu.VMEM((2, GATHER_ROWS, d // 2), src.dtype), pltpu.SemaphoreType.DMA((2,))],
        compiler_params=_params("arbitrary"),
        name="gather_rows",
    )(idx_blocks, idx_blocks, src)


def _moe_up_kernel(te_ref, nv_ref, x_ref, wg_ref, wu_ref, o_ref):
    valid = pl.program_id(1) < nv_ref[0]

    @pl.when(valid)
    def _():
        a = x_ref[...]
        g = jnp.dot(a, wg_ref[...], preferred_element_type=F32)
        u = jnp.dot(a, wu_ref[...], preferred_element_type=F32)
        o_ref[...] = (_silu(g) * u).astype(o_ref.dtype)

    @pl.when(jnp.logical_not(valid))
    def _():
        o_ref[...] = jnp.zeros_like(o_ref)


def moe_up(x_rows, tile_expert, n_valid, wg, wu, tn=1024):
    r, k = x_rows.shape
    f = wg.shape[2]
    tm = MOE_TILE
    w_spec = pl.BlockSpec((None, k, tn), lambda j, m, te, nv: (te[m], 0, j))
    return pl.pallas_call(
        _moe_up_kernel,
        out_shape=jax.ShapeDtypeStruct((r, f), BF16),
        grid_spec=pltpu.PrefetchScalarGridSpec(
            num_scalar_prefetch=2,
            grid=(f // tn, r // tm),
            in_specs=[pl.BlockSpec((tm, k), lambda j, m, te, nv: (m, 0)), w_spec, w_spec],
            out_specs=pl.BlockSpec((tm, tn), lambda j, m, te, nv: (m, j))),
        compiler_params=_params("parallel", "arbitrary"),
        name="moe_up",
    )(tile_expert, n_valid, x_rows, wg, wu)


def _moe_down_kernel(te_ref, nv_ref, a_ref, w_ref, o_ref):
    valid = pl.program_id(1) < nv_ref[0]

    @pl.when(valid)
    def _():
        o_ref[...] = _pack_halves(jnp.dot(a_ref[...], w_ref[...], preferred_element_type=F32))

    @pl.when(jnp.logical_not(valid))
    def _():
        o_ref[...] = jnp.zeros_like(o_ref)


MOE_DOWN_TILE = 2048


def moe_down(act_rows, tile_expert, n_valid, wd):
    r, f = act_rows.shape
    d = wd.shape[2]
    tm, tn = MOE_TILE, MOE_DOWN_TILE
    return pl.pallas_call(
        _moe_down_kernel,
        out_shape=jax.ShapeDtypeStruct((r, d // 2), jnp.uint32),
        grid_spec=pltpu.PrefetchScalarGridSpec(
            num_scalar_prefetch=2,
            grid=(d // tn, r // tm),
            in_specs=[pl.BlockSpec((tm, f), lambda j, m, te, nv: (m, 0)),
                      pl.BlockSpec((None, f, tn), lambda j, m, te, nv: (te[m], 0, j))],
            out_specs=pl.BlockSpec((tm, tn // 2), lambda j, m, te, nv: (m, j))),
        compiler_params=_params("parallel", "arbitrary"),
        name="moe_down",
    )(tile_expert, n_valid, act_rows, wd)


def _unpack_expert_rows(p):
    half = MOE_DOWN_TILE // 2
    parts = []
    for j in range(p.shape[1] // half):
        parts.extend(_unpack_halves(p[:, j * half:(j + 1) * half]))
    return jnp.concatenate(parts, axis=1)


def _moe_combine_kernel(pos1_ref, pos2_ref, nxt1_ref, nxt2_ref, h_ref, y_ref, wts_ref, g_ref, *refs, keep_x):
    out_refs, (ybuf_ref, sem) = refs[:-2], refs[-2:]
    tm = h_ref.shape[0]

    def step_dmas(nxt, slot, wait):
        first, second = (nxt1_ref, nxt2_ref) if nxt else (pos1_ref, pos2_ref)
        _row_dmas(first, y_ref, ybuf_ref.at[slot], sem.at[slot], tm, 0, wait)
        _row_dmas(second, y_ref, ybuf_ref.at[slot], sem.at[slot], tm, tm, wait)

    slot = _prefetched_rows(step_dmas)
    w = wts_ref[...]
    y1 = _unpack_expert_rows(ybuf_ref[slot, 0:tm, :])
    y2 = _unpack_expert_rows(ybuf_ref[slot, tm:2 * tm, :])
    x = h_ref[...] + w[:, 0:1] * y1 + w[:, 1:2] * y2
    if keep_x:
        out_refs[0][...] = x
    r = lax.rsqrt(jnp.mean(x * x, axis=-1, keepdims=True) + EPS)
    out_refs[-1][...] = (x * r * g_ref[...]).astype(out_refs[-1].dtype)


def moe_combine(h, y_rows, pos, wts, next_norm_g, normed_dtype, keep_x, tm=256):
    n, d = h.shape
    nt = n // tm
    row_spec = pl.BlockSpec((tm, d), lambda i: (i, 0))
    shapes = [jax.ShapeDtypeStruct((n, d), F32)] * keep_x + [jax.ShapeDtypeStruct((n, d), normed_dtype)]
    pos_blocks = pos.reshape(2 * nt, 1, tm)
    smem_block = functools.partial(pl.BlockSpec, (1, 1, tm), memory_space=pltpu.SMEM)
    nxt = lambda i: jnp.minimum(i + 1, nt - 1)
    return pl.pallas_call(
        functools.partial(_moe_combine_kernel, keep_x=keep_x),
        out_shape=tuple(shapes),
        grid=(nt,),
        in_specs=[smem_block(lambda i: (i, 0, 0)),
                  smem_block(lambda i: (i + nt, 0, 0)),
                  smem_block(lambda i: (nxt(i), 0, 0)),
                  smem_block(lambda i: (nxt(i) + nt, 0, 0)),
                  row_spec,
                  pl.BlockSpec(memory_space=pl.ANY),
                  pl.BlockSpec((tm, LANES), lambda i: (i, 0)),
                  pl.BlockSpec((1, d), lambda i: (0, 0))],
        out_specs=tuple([row_spec] * len(shapes)),
        scratch_shapes=[pltpu.VMEM((2, 2 * tm, y_rows.shape[1]), y_rows.dtype), pltpu.SemaphoreType.DMA((2,))],
        compiler_params=_params("arbitrary"),
        name="moe_combine",
    )(pos_blocks, pos_blocks, pos_blocks, pos_blocks, h, y_rows, wts, next_norm_g.reshape(1, d))


def moe_ffn(h, norm_g, w_router_pad, wg, wu, wd, next_norm_g, normed_dtype, keep_x):
    n = h.shape[0]
    n_tiles = (2 * n) // MOE_TILE + N_EXPERTS
    n2, idx, wts = rmsnorm_router(h, norm_g, w_router_pad)
    row_token, pos, tile_expert, n_valid = _route_plan(idx, n_tiles)
    x_rows = gather_rows_bf16(n2, row_token)
    act_rows = moe_up(x_rows, tile_expert, n_valid, wg, wu)
    y_rows = moe_down(act_rows, tile_expert, n_valid, wd)
    return moe_combine(h, y_rows, pos, wts, next_norm_g, normed_dtype, keep_x)


def _ffn_down_kernel(a_ref, w_ref, res_ref, o_ref):
    @pl.when(pl.program_id(2) == 0)
    def _():
        o_ref[...] = res_ref[...]

    o_ref[...] += jnp.dot(a_ref[...], w_ref[...], preferred_element_type=F32)


def ffn_down(a, w, res, tm=MM_TILE, tn=MM_TILE, tk=4096):
    m, k = a.shape
    n = w.shape[1]
    return pl.pallas_call(
        _ffn_down_kernel,
        out_shape=jax.ShapeDtypeStruct((m, n), F32),
        grid=(m // tm, n // tn, k // tk),
        in_specs=[pl.BlockSpec((tm, tk), lambda i, j, kk: (i, kk)),
                  pl.BlockSpec((tk, tn), lambda i, j, kk: (kk, j)),
                  pl.BlockSpec((tm, tn), lambda i, j, kk: (i, j))],
        out_specs=pl.BlockSpec((tm, tn), lambda i, j, kk: (i, j)),
        compiler_params=_params("parallel", "parallel", "arbitrary"),
        name="ffn_down",
    )(a, w, res)


def _alibi_slope(h):
    return float(2.0 ** (-8.0 * (h + 1) / SWA_Q_HEADS))


def _swa_bias_table():
    qi = np.arange(WINDOW)[:, None]
    kj = np.arange(2 * WINDOW)[None, :]
    dist = qi - kj + WINDOW
    inside = (dist >= 0) & (dist < WINDOW)
    slopes = np.array([_alibi_slope(h) for h in range(SWA_Q_HEADS)], np.float64)
    bias = -slopes[:, None, None] * dist[None].astype(np.float64)
    later = np.where(inside[None], bias, NEG_BIG)
    first = np.where((inside & (kj >= WINDOW))[None], bias, NEG_BIG)
    table = np.stack([first, later]).astype(np.float32)
    return table.reshape(2, SWA_KV_HEADS, SWA_GROUP * WINDOW, 2 * WINDOW)


def _swa_kernel(q_ref, kp_ref, kc_ref, vp_ref, vc_ref, bias_ref, sink_ref, g_ref, o_ref, acc_ref):
    nt = (((1,), (1,)), ((), ()))
    zeros_half = jnp.zeros((2 * WINDOW, SWA_HEAD_DIM), BF16)
    ones = jnp.ones((2 * WINDOW, LANES), BF16)
    pairs = SWA_GROUP // 2

    def padded(ref_prev, ref_cur, j):
        cols = slice(j * SWA_HEAD_DIM, (j + 1) * SWA_HEAD_DIM)
        x = jnp.concatenate([ref_prev[:, cols], ref_cur[:, cols]], axis=0)
        return jnp.concatenate([x, zeros_half], axis=1), jnp.concatenate([zeros_half, x], axis=1)

    scores, maxes = [], []
    for j in range(SWA_KV_HEADS):
        k_pads = padded(kp_ref, kc_ref, j)
        parts = []
        for p in range(pairs):
            lanes = slice((j * pairs + p) * LANES, (j * pairs + p + 1) * LANES)
            q_pair = q_ref[:, lanes] * jnp.asarray(SWA_HEAD_DIM ** -0.5, BF16)
            for k_pad in k_pads:
                parts.append(lax.dot_general(q_pair, k_pad, nt, preferred_element_type=F32))
        s = jnp.concatenate(parts, axis=0) + bias_ref[j]
        scores.append(s)
        maxes.append(jnp.maximum(jnp.max(s, axis=-1, keepdims=True), sink_ref[j]))

    probs, invs = [], []
    for j in range(SWA_KV_HEADS):
        m = maxes[j]
        e = jnp.exp(scores[j] - jnp.concatenate([m, m], axis=1)).astype(BF16)
        denom = jnp.dot(e, ones, preferred_element_type=F32) + jnp.exp(sink_ref[j] - m)
        probs.append(e)
        invs.append(1.0 / denom)

    for j in range(SWA_KV_HEADS):
        v_pads = padded(vp_ref, vc_ref, j)
        for p in range(pairs):
            lanes = slice((j * pairs + p) * LANES, (j * pairs + p + 1) * LANES)
            out_pair = None
            for half in range(2):
                rows = slice((2 * p + half) * WINDOW, (2 * p + half + 1) * WINDOW)
                o = jnp.dot(probs[j][rows], v_pads[half], preferred_element_type=F32) * invs[j][rows]
                out_pair = o if out_pair is None else out_pair + o
            acc_ref[:, lanes] = out_pair

    acc = acc_ref[...]
    r = lax.rsqrt(jnp.mean(acc * acc, axis=-1, keepdims=True) + EPS)
    o_ref[...] = (acc * r * g_ref[...]).astype(o_ref.dtype)


def swa_attention(proj, sinks, norm_g, batch, seq):
    nb = seq // WINDOW
    rows = lambda b, n: b * nb + n
    prev = lambda b, n: b * nb + jnp.maximum(n - 1, 0)
    kcol = COL_SK // SWA_KV_WIDTH
    vcol = COL_SV // SWA_KV_WIDTH
    gq = SWA_GROUP * WINDOW
    bias = jnp.asarray(_swa_bias_table())
    sink_rows = jnp.broadcast_to(jnp.repeat(sinks.astype(F32), WINDOW).reshape(SWA_KV_HEADS, gq, 1),
                                 (SWA_KV_HEADS, gq, LANES))
    return pl.pallas_call(
        _swa_kernel,
        out_shape=jax.ShapeDtypeStruct((batch * seq, SWA_WIDTH), BF16),
        grid=(batch, nb),
        in_specs=[pl.BlockSpec((WINDOW, SWA_WIDTH), lambda b, n: (rows(b, n), COL_SQ // SWA_WIDTH)),
                  pl.BlockSpec((WINDOW, SWA_KV_WIDTH), lambda b, n: (prev(b, n), kcol)),
                  pl.BlockSpec((WINDOW, SWA_KV_WIDTH), lambda b, n: (rows(b, n), kcol)),
                  pl.BlockSpec((WINDOW, SWA_KV_WIDTH), lambda b, n: (prev(b, n), vcol)),
                  pl.BlockSpec((WINDOW, SWA_KV_WIDTH), lambda b, n: (rows(b, n), vcol)),
                  pl.BlockSpec((None, SWA_KV_HEADS, gq, 2 * WINDOW),
                               lambda b, n: (jnp.minimum(n, 1), 0, 0, 0)),
                  pl.BlockSpec((SWA_KV_HEADS, gq, LANES), lambda b, n: (0, 0, 0)),
                  pl.BlockSpec((1, SWA_WIDTH), lambda b, n: (0, 0))],
        out_specs=pl.BlockSpec((WINDOW, SWA_WIDTH), lambda b, n: (rows(b, n), 0)),
        scratch_shapes=[pltpu.VMEM((WINDOW, SWA_WIDTH), F32)],
        compiler_params=_params("parallel", "parallel"),
        name="swa_attention",
    )(proj, proj, proj, proj, proj, bias, sink_rows, norm_g.reshape(1, SWA_WIDTH))


GLA_STEP_CHUNKS = 4


def _gla_kernel(q_ref, k_ref, v_ref, gg_ref, glr_ref, gw_ref, gb_ref, ng_ref, o_ref, st_ref):
    c = GLA_CHUNK
    nt = (((1,), (1,)), ((), ()))

    @pl.when(pl.program_id(1) == 0)
    def _():
        st_ref[...] = jnp.zeros_like(st_ref)

    ri = lax.broadcasted_iota(jnp.int32, (c, c), 0)
    ci = lax.broadcasted_iota(jnp.int32, (c, c), 1)
    causal = ri >= ci
    tri = jnp.where(causal, 1.0, 0.0).astype(BF16)

    scaled = []
    for n in range(GLA_STEP_CHUNKS):
        rows = slice(n * c, (n + 1) * c)
        logits = jnp.dot(glr_ref[rows, :], gw_ref[...], preferred_element_type=F32) + gb_ref[...]
        log_a = (jnp.minimum(logits, 0.0) - jnp.log(1.0 + jnp.exp(-jnp.abs(logits)))) * (1.0 / GLA_TAU)
        la_hi = log_a.astype(BF16)
        la_lo = (log_a - la_hi.astype(F32)).astype(BF16)
        b = (jnp.dot(tri, la_hi, preferred_element_type=F32)
             + jnp.dot(tri, la_lo, preferred_element_type=F32))
        b_last = b[c - 1:c, :]
        q = q_ref[rows, :].astype(F32)
        k = k_ref[rows, :].astype(F32)
        q_dec = (q * jnp.exp(b) * (GLA_DK ** -0.5)).astype(BF16)
        k_inv = (k * jnp.exp(-b)).astype(BF16)
        k_end = (k * jnp.exp(b_last - b)).astype(BF16)
        scaled.append((q_dec, k_inv, k_end, jnp.exp(b_last)))

    for h in range(GLA_HEADS):
        kc = slice(h * GLA_DK, (h + 1) * GLA_DK)
        vc = slice(h * GLA_DV, (h + 1) * GLA_DV)
        st = st_ref[h]
        for n in range(GLA_STEP_CHUNKS):
            rows = slice(n * c, (n + 1) * c)
            q_dec, k_inv, k_end, dec = scaled[n]
            v = v_ref[rows, vc]
            a = lax.dot_general(q_dec[:, kc], k_inv[:, kc], nt, preferred_element_type=F32)
            a = jnp.where(causal, a, 0.0).astype(BF16)
            o = jnp.dot(a, v, preferred_element_type=F32)
            o += lax.dot_general(q_dec[:, kc], st.astype(BF16), nt, preferred_element_type=F32)
            upd = lax.dot_general(v, k_end[:, kc], (((0,), (0,)), ((), ())),
                                  preferred_element_type=F32)
            st = st * dec[:, kc] + upd
            r = lax.rsqrt(jnp.mean(o * o, axis=-1, keepdims=True) + EPS)
            gate = _silu(gg_ref[rows, vc].astype(F32))
            o_ref[rows, vc] = (o * r * ng_ref[...] * gate).astype(o_ref.dtype)
        st_ref[h] = st


def gla_attention(proj, gate_w_pad, gate_b, norm_g, batch, seq):
    c = GLA_CHUNK * GLA_STEP_CHUNKS
    nc = seq // c
    rows = lambda b, n: b * nc + n
    return pl.pallas_call(
        _gla_kernel,
        out_shape=jax.ShapeDtypeStruct((batch * seq, GLA_WIDTH), BF16),
        grid=(batch, nc),
        in_specs=[pl.BlockSpec((c, GLA_KEY_WIDTH), lambda b, n: (rows(b, n), COL_GQ // GLA_KEY_WIDTH)),
                  pl.BlockSpec((c, GLA_KEY_WIDTH), lambda b, n: (rows(b, n), COL_GK // GLA_KEY_WIDTH)),
                  pl.BlockSpec((c, GLA_WIDTH), lambda b, n: (rows(b, n), COL_GV // GLA_WIDTH)),
                  pl.BlockSpec((c, GLA_WIDTH), lambda b, n: (rows(b, n), COL_GG // GLA_WIDTH)),
                  pl.BlockSpec((c, LANES), lambda b, n: (rows(b, n), COL_GLR // LANES)),
                  pl.BlockSpec((LANES, GLA_KEY_WIDTH), lambda b, n: (0, 0)),
                  pl.BlockSpec((1, GLA_KEY_WIDTH), lambda b, n: (0, 0)),
                  pl.BlockSpec((1, GLA_DV), lambda b, n: (0, 0))],
        out_specs=pl.BlockSpec((c, GLA_WIDTH), lambda b, n: (rows(b, n), 0)),
        scratch_shapes=[pltpu.VMEM((GLA_HEADS, GLA_DV, GLA_DK), F32)],
        compiler_params=_params("parallel", "arbitrary"),
        name="gla_attention",
    )(proj, proj, proj, proj, proj, gate_w_pad, gate_b.reshape(1, GLA_KEY_WIDTH),
      norm_g.reshape(1, GLA_DV))


CAST_BLOCK_BYTES = 8 * 1024 * 1024


def _cast_kernel(w_ref, o_ref):
    o_ref[...] = w_ref[...].astype(o_ref.dtype)


def cast_weights(w, first, count):
    _, r, c = w.shape
    tr = max(8, min(r, CAST_BLOCK_BYTES // (4 * c)))
    return pl.pallas_call(
        _cast_kernel,
        out_shape=jax.ShapeDtypeStruct((count, r, c), BF16),
        grid=(count, r // tr),
        in_specs=[pl.BlockSpec((None, tr, c), lambda e, i: (first + e, i, 0))],
        out_specs=pl.BlockSpec((None, tr, c), lambda e, i: (e, i, 0)),
        compiler_params=_params("parallel", "parallel"),
        name="cast_weights",
    )(w)


_W_IN_SEGMENTS = ((0, 2048, COL_SQ), (2048, 2304, COL_SK), (2304, 2560, COL_SV), (2560, 3584, COL_GQ),
                  (3584, 4608, COL_GK), (4608, 6656, COL_GV), (6656, 8704, COL_GG))
IN_COLS = 8720
PREP_COLS = 256
assert PROJ_COLS == COL_GLR + PREP_COLS


def _w_in_source_blocks():
    table = np.zeros((PROJ_COLS // PREP_COLS,), np.int32)
    for lo, hi, dst in _W_IN_SEGMENTS:
        for k in range((hi - lo) // PREP_COLS):
            table[dst // PREP_COLS + k] = lo // PREP_COLS + k
    return table


def _prep_w_in_kernel(tbl_ref, wt_ref, tail_ref, o_ref):
    jb = pl.program_id(0)
    full_blocks = COL_GLR // PREP_COLS

    @pl.when(jb < full_blocks)
    def _():
        o_ref[...] = wt_ref[...].T.astype(o_ref.dtype)

    @pl.when(jb == full_blocks)
    def _():
        o_ref[:, 0:LANES] = tail_ref[...].T.astype(o_ref.dtype)
        o_ref[:, LANES:PREP_COLS] = jnp.zeros((o_ref.shape[0], PREP_COLS - LANES), o_ref.dtype)


def prep_w_in(w_in, layer):
    _, d, _ = w_in.shape
    w_t = jnp.transpose(w_in, (0, 2, 1))
    tail = jnp.pad(w_t[layer, COL_GLR:IN_COLS], ((0, LANES - GLA_GATE_RANK), (0, 0)))
    return pl.pallas_call(
        _prep_w_in_kernel,
        out_shape=jax.ShapeDtypeStruct((d, PROJ_COLS), BF16),
        grid_spec=pltpu.PrefetchScalarGridSpec(
            num_scalar_prefetch=1,
            grid=(PROJ_COLS // PREP_COLS,),
            in_specs=[pl.BlockSpec((None, PREP_COLS, d), lambda j, tbl: (layer, tbl[j], 0)),
                      pl.BlockSpec((LANES, d), lambda j, tbl: (0, 0))],
            out_specs=pl.BlockSpec((d, PREP_COLS), lambda j, tbl: (0, j))),
        compiler_params=_params("arbitrary"),
        name="prep_w_in",
    )(jnp.asarray(_w_in_source_blocks()), w_t, tail)


def kernel(x, attn_norm_g, w_in, gla_gate_w, gla_gate_b, gla_norm_g, swa_sinks, swa_norm_g, w_out,
           ffn_norm_g, dense_w_gate, dense_w_up, dense_w_down, moe_router, moe_w_gate, moe_w_up,
           moe_w_down, final_norm_g):
    batch, seq, d = x.shape
    xs = x.reshape(batch * seq, d)
    moe_shape = (DEPTH // 2 * N_EXPERTS, d, D_FF_EXPERT)
    n = rmsnorm(xs, attn_norm_g[0], BF16)
    for layer in range(DEPTH):
        proj = in_proj(n, prep_w_in(w_in, layer))
        swa = swa_attention(proj, swa_sinks[layer], swa_norm_g[layer], batch, seq)
        gate_w_pad = jnp.pad(gla_gate_w[layer], ((0, LANES - GLA_GATE_RANK), (0, 0))).astype(BF16)
        gla = gla_attention(proj, gate_w_pad, gla_gate_b[layer], gla_norm_g[layer], batch, seq)
        h = out_proj(swa, gla, cast_weights(w_out, layer, 1)[0], xs)
        j = layer // 2
        if layer % 2 == 0:
            n2 = rmsnorm(h, ffn_norm_g[layer], BF16)
            act = ffn_up(n2, cast_weights(dense_w_gate, j, 1)[0], cast_weights(dense_w_up, j, 1)[0])
            xs = ffn_down(act, cast_weights(dense_w_down, j, 1)[0], h)
            n = rmsnorm(xs, attn_norm_g[layer + 1], BF16)
        else:
            wr = jnp.pad(moe_router[j], ((0, 0), (0, LANES - N_EXPERTS))).astype(BF16)
            wg = cast_weights(moe_w_gate.reshape(moe_shape), j * N_EXPERTS, N_EXPERTS)
            wu = cast_weights(moe_w_up.reshape(moe_shape), j * N_EXPERTS, N_EXPERTS)
            wd = cast_weights(moe_w_down.reshape(moe_shape[0], D_FF_EXPERT, d), j * N_EXPERTS, N_EXPERTS)
            if layer + 1 < DEPTH:
                xs, n = moe_ffn(h, ffn_norm_g[layer], wr, wg, wu, wd, attn_norm_g[layer + 1], BF16, True)
            else:
                (out,) = moe_ffn(h, ffn_norm_g[layer], wr, wg, wu, wd, final_norm_g, F32, False)
    return out.reshape(batch, seq, d)
```
